```python
import math, functools
import jax, jax.numpy as jnp
from jax import lax
import numpy as np

D_MODEL = 1024
BATCH = 1
SEQ = 16384
DEPTH = 1
DEC_BATCH = 128
DEC_SEQ = 4
PAST_LEN = 8192
PAGE_SIZE = 128

GLA_HEADS = 4
GLA_DK = 64
GLA_DV = 128
GLA_GATE_RANK = 16
GLA_GATE_NORM = 16.0
GLA_CHUNK = 64
GLA_QK = GLA_HEADS * GLA_DK
GLA_W = GLA_HEADS * GLA_DV
DSA_HEADS = 4
DSA_DH = 128
DSA_W = DSA_HEADS * DSA_DH
IDX_HEADS = 8
IDX_DH = 64
IDX_W = IDX_HEADS * IDX_DH
DSA_TOPK_MAX = 256
DSA_QBLOCK = 128
REL_BUCKETS = 32
REL_MAX_DIST = 128
PEER_HEADS = 8
PEER_NKEYS = 128
PEER_DKEY = 128
PEER_TOPK = 16
PEER_BLOCK = 128
N_EXPERTS = PEER_NKEYS * PEER_NKEYS
EPS = 1e-6
SPLIT_SIZES = (GLA_QK, GLA_QK, GLA_W, GLA_W, GLA_GATE_RANK,
               DSA_W, DSA_W, DSA_W, IDX_W, IDX_DH, IDX_HEADS)
D_IN = sum(SPLIT_SIZES)
MIX_W = GLA_W + DSA_W

kernel_name = "hymba_gla_dsa_peer_adaln_step"


def rms_norm(x, g):
    xf = x.astype(jnp.float32)
    y = xf * lax.rsqrt(jnp.mean(xf * xf, axis=-1, keepdims=True) + EPS)
    return (y * g.astype(jnp.float32)).astype(x.dtype)


def t5_bucket(dist):
    max_exact = REL_BUCKETS // 2
    d = jnp.maximum(dist, 0)
    large = max_exact + (jnp.log(jnp.maximum(d, 1).astype(jnp.float32) / max_exact)
                         / math.log(REL_MAX_DIST / max_exact)
                         * (REL_BUCKETS - max_exact)).astype(jnp.int32)
    large = jnp.minimum(large, REL_BUCKETS - 1)
    return jnp.where(d < max_exact, d, large)


def gla_chunked(q, k, v, lg, s0, chunk):
    B, T, H = q.shape[:3]
    n = T // chunk
    f32 = jnp.float32

    def blocks(a):
        return jnp.moveaxis(a.astype(f32).reshape(B, n, chunk, *a.shape[2:]), 1, 0)

    causal = jnp.tril(jnp.ones((chunk, chunk), bool))[None, :, :, None, None]

    def step(S, inp):
        qc, kc, vc, gc = inp
        b = jnp.cumsum(gc, axis=1)
        o_inter = jnp.einsum('bchk,bhkv->bchv', qc * jnp.exp(b), S)
        diff = b[:, :, None] - b[:, None, :]
        decay = jnp.exp(jnp.where(causal, diff, -jnp.inf))
        A = jnp.einsum('bihk,bjhk,bijhk->bhij', qc, kc, decay)
        o_intra = jnp.einsum('bhij,bjhv->bihv', A, vc)
        b_last = b[:, -1]
        S_new = jnp.exp(b_last)[..., None] * S + jnp.einsum(
            'bjhk,bjhv->bhkv', kc * jnp.exp(b_last[:, None] - b), vc)
        return S_new, o_inter + o_intra

    S, o = lax.scan(step, s0.astype(f32), (blocks(q), blocks(k), blocks(v), blocks(lg)))
    o = jnp.moveaxis(o, 0, 1).reshape(B, T, H, v.shape[-1])
    return o, S


def gla_mix(q, k, v, g, a_low, s0, chunk, w_a2, b_a2, g_out):
    B, T = q.shape[:2]
    q = q.reshape(B, T, GLA_HEADS, GLA_DK) * (GLA_DK ** -0.5)
    k = k.reshape(B, T, GLA_HEADS, GLA_DK)
    v = v.reshape(B, T, GLA_HEADS, GLA_DV)
    lg = jax.nn.log_sigmoid((a_low @ w_a2 + b_a2).astype(jnp.float32)) / GLA_GATE_NORM
    lg = lg.reshape(B, T, GLA_HEADS, GLA_DK)
    o, S = gla_chunked(q, k, v, lg, s0, chunk)
    o = rms_norm(o, g_out).reshape(B, T, GLA_W).astype(g.dtype) * jax.nn.silu(g)
    return o, S


def indexer_topk(qi, wi, kidx, qpos, topk):
    f32 = jnp.float32
    s = jnp.einsum('bthd,bsd->bths', qi.astype(f32), kidx.astype(f32)) * (IDX_DH ** -0.5)
    score = jnp.einsum('bths,bth->bts', jax.nn.relu(s), wi.astype(f32) * (IDX_HEADS ** -0.5))
    L = kidx.shape[1]
    admissible = jnp.arange(L)[None, None, :] <= qpos[None, :, None]
    score = jnp.where(admissible, score, -jnp.inf)
    _, idx = lax.top_k(score, topk)
    return idx


def sparse_attend(q, k_sel, v_sel, qpos, kpos, t5_bias):
    f32 = jnp.float32
    logits = jnp.einsum('bthd,btkhd->bthk', q.astype(f32), k_sel.astype(f32)) * (DSA_DH ** -0.5)
    dist = qpos[None, :, None] - kpos
    bias = jnp.moveaxis(t5_bias.astype(f32)[t5_bucket(dist)], -1, 2)
    valid = (dist >= 0)[:, :, None, :]
    p = jax.nn.softmax(jnp.where(valid, logits + bias, -jnp.inf), axis=-1)
    return jnp.einsum('bthk,btkhd->bthd', p, v_sel.astype(f32)).astype(q.dtype)


def dsa_prompt(q, k, v, qi, ki, wi, t5_bias):
    B, T = q.shape[:2]
    topk = min(DSA_TOPK_MAX, T // 4)
    qb = min(DSA_QBLOCK, T)
    nb = T // qb
    bidx = jnp.arange(B)[:, None, None]

    def blk(a):
        return jnp.moveaxis(a.reshape(B, nb, qb, *a.shape[2:]), 1, 0)

    def one(inp):
        q_b, qi_b, wi_b, t0 = inp
        qpos = t0 + jnp.arange(qb)
        idx = indexer_topk(qi_b, wi_b, ki, qpos, topk)
        return sparse_attend(q_b, k[bidx, idx], v[bidx, idx], qpos, idx, t5_bias)

    o = lax.map(one, (blk(q), blk(qi), blk(wi), jnp.arange(nb) * qb))
    return jnp.moveaxis(o, 0, 1).reshape(B, T, DSA_HEADS, DSA_DH)


def dsa_sample(q, k, v, qi, ki, wi, cache_k, cache_v, cache_kidx, page_table, t5_bias):
    Bd, T = q.shape[:2]
    n_pages = page_table.shape[1]
    ps = cache_k.shape[1]
    past = n_pages * ps
    kidx_past = cache_kidx[page_table].reshape(Bd, past, IDX_DH)
    kidx_all = jnp.concatenate([kidx_past, ki.astype(kidx_past.dtype)], axis=1)
    qpos = past + jnp.arange(T)
    topk = min(DSA_TOPK_MAX, (past + T) // 4)
    idx = indexer_topk(qi, wi, kidx_all, qpos, topk)
    bidx = jnp.arange(Bd)[:, None, None]
    in_past = (idx < past)[..., None, None]
    pi = jnp.minimum(idx, past - 1)
    phys = page_table[bidx, pi // ps]
    off = pi % ps
    ni = jnp.clip(idx - past, 0, T - 1)
    k_sel = jnp.where(in_past, cache_k[phys, off], k[bidx, ni].astype(cache_k.dtype))
    v_sel = jnp.where(in_past, cache_v[phys, off], v[bidx, ni].astype(cache_v.dtype))
    return sparse_attend(q, k_sel, v_sel, qpos, idx, t5_bias)


def token_mix(h, w_in, w_a2, b_a2, g_gla, g_dsa, w_out, gla_s0, gla_chunk, dsa_fn):
    B, T = h.shape[:2]
    points = [int(i) for i in np.cumsum(SPLIT_SIZES)[:-1]]
    gq, gk, gv, gg, ga, dq, dk, dv, qi, ki, wi = jnp.split(h @ w_in, points, axis=-1)
    o_gla, s_gla = gla_mix(gq, gk, gv, gg, ga, gla_s0, gla_chunk, w_a2, b_a2, g_gla)
    dq = dq.reshape(B, T, DSA_HEADS, DSA_DH)
    dk = dk.reshape(B, T, DSA_HEADS, DSA_DH)
    dv = dv.reshape(B, T, DSA_HEADS, DSA_DH)
    qi = qi.reshape(B, T, IDX_HEADS, IDX_DH)
    o_dsa = dsa_fn(dq, dk, dv, qi, ki, wi)
    o_dsa = rms_norm(o_dsa.reshape(B, T, DSA_W), g_dsa)
    y = jnp.concatenate([o_gla.astype(h.dtype), o_dsa.astype(h.dtype)], axis=-1) @ w_out
    return y, (dk, dv, ki, s_gla)


def peer(h, w_q, keys1, keys2, u_tab, v_tab):
    shape = h.shape
    x = h.reshape(-1, shape[-1])
    n = x.shape[0]
    nb = -(-n // PEER_BLOCK)
    xp = jnp.pad(x, ((0, nb * PEER_BLOCK - n), (0, 0))).reshape(nb, PEER_BLOCK, shape[-1])
    half = PEER_DKEY // 2
    f32 = jnp.float32

    def one(xb):
        q = (xb @ w_q).reshape(-1, PEER_HEADS, PEER_DKEY).astype(f32)
        s1 = jnp.einsum('nhd,hkd->nhk', q[..., :half], keys1.astype(f32))
        s2 = jnp.einsum('nhd,hkd->nhk', q[..., half:], keys2.astype(f32))
        v1, i1 = lax.top_k(s1, PEER_TOPK)
        v2, i2 = lax.top_k(s2, PEER_TOPK)
        cand = (v1[..., :, None] + v2[..., None, :]).reshape(-1, PEER_HEADS, PEER_TOPK * PEER_TOPK)
        cid = (i1[..., :, None] * PEER_NKEYS + i2[..., None, :]).reshape(-1, PEER_HEADS, PEER_TOPK * PEER_TOPK)
        sc, j = lax.top_k(cand, PEER_TOPK)
        eid = jnp.take_along_axis(cid, j, axis=-1)
        gate = jax.nn.softmax(sc, axis=-1)
        act = jax.nn.gelu(jnp.einsum('nd,nhkd->nhk', xb.astype(f32), u_tab[eid].astype(f32)),
                          approximate=False)
        return jnp.einsum('nhk,nhkd->nd', (gate * act).astype(xb.dtype), v_tab[eid])

    out = lax.map(one, xp).reshape(nb * PEER_BLOCK, shape[-1])[:n]
    return out.reshape(shape)


def layer(x, c, lw, gla_s0, gla_chunk, dsa_fn):
    (w_ada, b_ada, g1, g2, w_in, w_a2, b_a2, g_gla, g_dsa, w_out,
     w_pq, pk1, pk2, pu, pv) = lw
    mod = jax.nn.silu(c) @ w_ada + b_ada
    sh1, sc1, gt1, sh2, sc2, gt2 = jnp.split(mod[:, None, :], 6, axis=-1)
    h = rms_norm(x, g1) * (1 + sc1) + sh1
    y, st = token_mix(h, w_in, w_a2, b_a2, g_gla, g_dsa, w_out, gla_s0, gla_chunk, dsa_fn)
    x = x + gt1 * y
    h = rms_norm(x, g2) * (1 + sc2) + sh2
    x = x + gt2 * peer(h, w_pq, pk1, pk2, pu, pv)
    return x, st


def setup_inputs(seed: int = 0) -> dict:
    key = jax.random.key(seed)
    ks = jax.random.split(key, 32)
    f32 = jnp.float32

    def nrm(k, shape, s):
        return jax.random.normal(k, shape, f32) * s

    n_pages = PAST_LEN // PAGE_SIZE
    n_used = DEC_BATCH * n_pages
    n_phys = n_used + max(1, n_used // 4)
    page_table = jax.random.permutation(ks[0], n_phys)[:n_used].reshape(DEC_BATCH, n_pages).astype(jnp.int32)
    D = D_MODEL
    return {
        "x_prompt": nrm(ks[1], (BATCH, SEQ, D), 1.0),
        "x_sample": nrm(ks[2], (DEC_BATCH, DEC_SEQ, D), 1.0),
        "cache_k": nrm(ks[3], (DEPTH, n_phys, PAGE_SIZE, DSA_HEADS, DSA_DH), 1.0),
        "cache_v": nrm(ks[4], (DEPTH, n_phys, PAGE_SIZE, DSA_HEADS, DSA_DH), 1.0),
        "cache_kidx": nrm(ks[5], (DEPTH, n_phys, PAGE_SIZE, IDX_DH), 1.0),
        "state_gla": nrm(ks[6], (DEPTH, DEC_BATCH, GLA_HEADS, GLA_DK, GLA_DV), 1.0),
        "page_table": page_table,
        "c_prompt": nrm(ks[7], (BATCH, D), 1.0),
        "c_sample": nrm(ks[8], (DEC_BATCH, D), 1.0),
        "w_ada": nrm(ks[9], (DEPTH, D, 6 * D), 0.5 * D ** -0.5),
        "b_ada": nrm(ks[10], (DEPTH, 6 * D), 0.01),
        "g_norm1": 1.0 + nrm(ks[11], (DEPTH, D), 0.01),
        "g_norm2": 1.0 + nrm(ks[12], (DEPTH, D), 0.01),
        "w_in": nrm(ks[13], (DEPTH, D, D_IN), D ** -0.5),
        "w_gate_a2": nrm(ks[14], (DEPTH, GLA_GATE_RANK, GLA_QK), GLA_GATE_RANK ** -0.5),
        "b_gate_a2": nrm(ks[15], (DEPTH, GLA_QK), 0.01),
        "g_gla_out": 1.0 + nrm(ks[16], (DEPTH, GLA_DV), 0.01),
        "g_dsa_out": 1.0 + nrm(ks[17], (DEPTH, DSA_W), 0.01),
        "w_out": nrm(ks[18], (DEPTH, MIX_W, D), MIX_W ** -0.5),
        "t5_bias": nrm(ks[19], (REL_BUCKETS, DSA_HEADS), 0.5),
        "w_peer_q": nrm(ks[20], (DEPTH, D, PEER_HEADS * PEER_DKEY), D ** -0.5),
        "peer_keys1": nrm(ks[21], (DEPTH, PEER_HEADS, PEER_NKEYS, PEER_DKEY // 2), (PEER_DKEY // 2) ** -0.5),
        "peer_keys2": nrm(ks[22], (DEPTH, PEER_HEADS, PEER_NKEYS, PEER_DKEY // 2), (PEER_DKEY // 2) ** -0.5),
        "peer_u": nrm(ks[23], (DEPTH, N_EXPERTS, D), D ** -0.5),
        "peer_v": nrm(ks[24], (DEPTH, N_EXPERTS, D), PEER_HEADS ** -0.5),
        "g_final": 1.0 + nrm(ks[25], (D,), 0.01),
    }


def reference(x_prompt, x_sample, cache_k, cache_v, cache_kidx, state_gla, page_table,
              c_prompt, c_sample, w_ada, b_ada, g_norm1, g_norm2, w_in, w_gate_a2, b_gate_a2,
              g_gla_out, g_dsa_out, w_out, t5_bias, w_peer_q, peer_keys1, peer_keys2,
              peer_u, peer_v, g_final):
    xp, xs = x_prompt, x_sample
    kp_l, vp_l, kip_l, sp_l = [], [], [], []
    ks_l, vs_l, kis_l, ss_l = [], [], [], []
    for l in range(DEPTH):
        lw = (w_ada[l], b_ada[l], g_norm1[l], g_norm2[l], w_in[l], w_gate_a2[l], b_gate_a2[l],
              g_gla_out[l], g_dsa_out[l], w_out[l], w_peer_q[l], peer_keys1[l], peer_keys2[l],
              peer_u[l], peer_v[l])
        s0p = jnp.zeros((xp.shape[0], GLA_HEADS, GLA_DK, GLA_DV), jnp.float32)
        xp, (kp, vp, kip, sp) = layer(
            xp, c_prompt, lw, s0p, min(GLA_CHUNK, xp.shape[1]),
            functools.partial(dsa_prompt, t5_bias=t5_bias))
        xs, (ksm, vsm, kism, ssm) = layer(
            xs, c_sample, lw, state_gla[l], xs.shape[1],
            functools.partial(dsa_sample, cache_k=cache_k[l], cache_v=cache_v[l],
                              cache_kidx=cache_kidx[l], page_table=page_table, t5_bias=t5_bias))
        kp_l.append(kp); vp_l.append(vp); kip_l.append(kip); sp_l.append(sp)
        ks_l.append(ksm); vs_l.append(vsm); kis_l.append(kism); ss_l.append(ssm)
    y_prompt = rms_norm(xp, g_final)
    y_sample = rms_norm(xs, g_final)
    return (y_prompt, y_sample,
            jnp.stack(kp_l), jnp.stack(vp_l), jnp.stack(kip_l), jnp.stack(sp_l),
            jnp.stack(ks_l), jnp.stack(vs_l), jnp.stack(kis_l), jnp.stack(ss_l))
```

```python
import functools
import math

import numpy as np
import jax
import jax.numpy as jnp
from jax import lax
from jax.experimental import pallas as pl
from jax.experimental.pallas import tpu as pltpu

F32 = jnp.float32
BF16 = jnp.bfloat16
I32 = jnp.int32

EPS = 1e-6
GLA_HEADS, GLA_DK, GLA_DV = 4, 64, 128
GLA_GATE_RANK = 16
GLA_GATE_NORM = 16.0
GLA_QK = GLA_HEADS * GLA_DK
GLA_W = GLA_HEADS * GLA_DV
DSA_HEADS, DSA_DH = 4, 128
DSA_W = DSA_HEADS * DSA_DH
IDX_HEADS, IDX_DH = 8, 64
IDX_W = IDX_HEADS * IDX_DH
DSA_TOPK_MAX = 256
REL_BUCKETS, REL_MAX_DIST = 32, 128
PEER_HEADS, PEER_NKEYS, PEER_DKEY, PEER_TOPK = 8, 128, 128, 16
SPLIT_SIZES = (GLA_QK, GLA_QK, GLA_W, GLA_W, GLA_GATE_RANK,
               DSA_W, DSA_W, DSA_W, IDX_W, IDX_DH, IDX_HEADS)

LANE = 128
NEG = -1e30
VMEM_LIMIT = 56 * 1024 * 1024

_PAD_SIZES = tuple(-(-s // LANE) * LANE for s in SPLIT_SIZES)
_PAD_OFFS = tuple(int(v) for v in np.cumsum((0,) + _PAD_SIZES))
(O_GQ, O_GK, O_GV, O_GG, O_GA, O_DQ, O_DK, O_DV, O_QI, O_KI, O_WI, D_IN_PAD) = _PAD_OFFS


def _cparams(sem):
    return pltpu.CompilerParams(dimension_semantics=sem, vmem_limit_bytes=VMEM_LIMIT)


def _dot(a, b):
    return jnp.dot(a, b, preferred_element_type=F32)


def _dot_nt(a, b):
    return lax.dot_general(a, b, (((1,), (1,)), ((), ())), preferred_element_type=F32)


def _rms(x, g):
    return x * lax.rsqrt(jnp.mean(x * x, axis=-1, keepdims=True) + EPS) * g


def _ada_kernel(c_ref, w_ref, b_ref, o_ref):
    c = c_ref[...]
    s = c * jax.nn.sigmoid(c)
    o_ref[...] = _dot(s.astype(BF16), w_ref[...].astype(BF16)) + b_ref[...]


def _ada(c, w, b):
    m, d = c.shape
    n = w.shape[1]
    tn = 1024
    return pl.pallas_call(
        _ada_kernel,
        grid=(n // tn,),
        in_specs=[pl.BlockSpec((m, d), lambda j: (0, 0)),
                  pl.BlockSpec((d, tn), lambda j: (0, j)),
                  pl.BlockSpec((1, tn), lambda j: (0, j))],
        out_specs=pl.BlockSpec((m, tn), lambda j: (0, j)),
        out_shape=jax.ShapeDtypeStruct((m, n), F32),
        compiler_params=_cparams(("arbitrary",)),
        name="ada",
    )(c, w, b.reshape(1, n))


def _proj_kernel(x_ref, g_ref, sc_ref, sh_ref, w_ref, wa2_ref, ba2_ref,
                 q_ref, k_ref, v_ref, gg_ref, lg_ref, dq_ref, dk_ref, dv_ref,
                 dkb_ref, dvb_ref, qi_ref, ki_ref, kib_ref, wi_ref):
    x = x_ref[...]
    h = _rms(x, g_ref[...]) * (1.0 + sc_ref[...]) + sh_ref[...]
    p = _dot(h.astype(BF16), w_ref[...])
    q_ref[...] = p[:, O_GQ:O_GK] * (GLA_DK ** -0.5)
    k_ref[...] = p[:, O_GK:O_GV]
    v_ref[...] = p[:, O_GV:O_GG]
    gg_ref[...] = p[:, O_GG:O_GA]
    z = _dot(p[:, O_GA:O_DQ].astype(BF16), wa2_ref[...]) + ba2_ref[...]
    lg_ref[...] = (jnp.minimum(z, 0.0) - jnp.log(1.0 + jnp.exp(-jnp.abs(z)))) * (1.0 / GLA_GATE_NORM)
    dq_ref[...] = (p[:, O_DQ:O_DK] * (DSA_DH ** -0.5)).astype(BF16)
    dk = p[:, O_DK:O_DV]
    dv = p[:, O_DV:O_QI]
    dk_ref[...] = dk
    dv_ref[...] = dv
    dkb_ref[...] = dk.astype(BF16)
    dvb_ref[...] = dv.astype(BF16)
    qi_ref[...] = (p[:, O_QI:O_KI] * (IDX_DH ** -0.5)).astype(BF16)
    ki = p[:, O_KI:O_KI + IDX_DH]
    ki_ref[...] = ki
    kib_ref[...] = ki.astype(BF16)
    wi_ref[...] = p[:, O_WI:D_IN_PAD] * (IDX_HEADS ** -0.5)


def _proj(x, g1, sc, sh, w_in_p, wa2_p, ba2, tm):
    r, d = x.shape
    rows_mod = sc.shape[0]
    mod_spec = (pl.BlockSpec((1, d), lambda i: (0, 0)) if rows_mod == 1
                else pl.BlockSpec((tm, d), lambda i: (i, 0)))
    full = lambda shp: pl.BlockSpec(shp, lambda i: (0,) * len(shp))
    widths = [(GLA_QK, F32), (GLA_QK, F32), (GLA_W, F32), (GLA_W, F32), (GLA_QK, F32),
              (DSA_W, BF16), (DSA_W, F32), (DSA_W, F32), (DSA_W, BF16), (DSA_W, BF16),
              (IDX_W, BF16), (IDX_DH, F32), (IDX_DH, BF16), (LANE, F32)]
    return pl.pallas_call(
        _proj_kernel,
        grid=(r // tm,),
        in_specs=[pl.BlockSpec((tm, d), lambda i: (i, 0)), full((1, d)), mod_spec, mod_spec,
                  full(w_in_p.shape), full(wa2_p.shape), full((1, GLA_QK))],
        out_specs=[pl.BlockSpec((tm, w), lambda i: (i, 0)) for w, _ in widths],
        out_shape=[jax.ShapeDtypeStruct((r, w), dt) for w, dt in widths],
        compiler_params=_cparams(("arbitrary",)),
        name="proj",
    )(x, g1.reshape(1, d), sc, sh, w_in_p, wa2_p, ba2.reshape(1, GLA_QK))


GLA_SUB = 16


def _gla_kernel(q_ref, k_ref, v_ref, gg_ref, lg_ref, go_ref, s0_ref, o_ref, s_ref, st_scr):
    c = pl.program_id(1)

    @pl.when(c == 0)
    def _():
        st_scr[...] = s0_ref[0]

    q = q_ref[0]
    k = k_ref[0]
    v = v_ref[0]
    lg = lg_ref[0]
    C = q.shape[0]
    rows = lax.broadcasted_iota(I32, (C, C), 0)
    cols = lax.broadcasted_iota(I32, (C, C), 1)
    tri = (rows >= cols).astype(F32)
    b = jnp.dot(tri, lg, preferred_element_type=F32, precision=lax.Precision.HIGHEST)
    b_last = b[C - 1:C, :]
    qc = q * jnp.exp(b)
    kc = k * jnp.exp(b_last - b)
    kcT = kc.T
    dec_col = jnp.broadcast_to(jnp.exp(jnp.sum(lg.T, axis=1, keepdims=True)), (GLA_QK, GLA_DV))
    go = go_ref[...]
    outs = []
    for h in range(GLA_HEADS):
        sl = slice(h * GLA_DK, (h + 1) * GLA_DK)
        vh = v[:, h * GLA_DV:(h + 1) * GLA_DV].astype(BF16)
        S = st_scr[h]
        o = _dot(qc[:, sl].astype(BF16), S.astype(BF16))
        o_rows = []
        sub = min(GLA_SUB, C)
        for blk in range(C // sub):
            r0, r1 = blk * sub, (blk + 1) * sub
            bref = b[r0 - 1:r0, sl] if blk > 0 else jnp.zeros((1, GLA_DK), F32)
            qt = q[r0:r1, sl] * jnp.exp(b[r0:r1, sl] - bref)
            kt = k[:r1, sl] * jnp.exp(bref - b[:r1, sl])
            a = _dot_nt(qt.astype(BF16), kt.astype(BF16))
            causal = (lax.broadcasted_iota(I32, (sub, r1), 1)
                      <= lax.broadcasted_iota(I32, (sub, r1), 0) + r0)
            a = jnp.where(causal, a, 0.0)
            o_rows.append(_dot(a.astype(BF16), vh[:r1]))
        o = o + (jnp.concatenate(o_rows, axis=0) if len(o_rows) > 1 else o_rows[0])
        st_scr[h] = dec_col[sl] * S + _dot(kcT[sl].astype(BF16), vh)
        outs.append(_rms(o, go))
    o_all = jnp.concatenate(outs, axis=1)
    gg = gg_ref[0]
    o_ref[0] = (o_all * (gg * jax.nn.sigmoid(gg))).astype(BF16)

    @pl.when(c == pl.num_programs(1) - 1)
    def _():
        s_ref[0] = st_scr[...]


def _gla(q, k, v, gg, lg, g_out, s0, chunk):
    B, T, _ = q.shape
    n = T // chunk
    tok = lambda w: pl.BlockSpec((1, chunk, w), lambda b, c: (b, c, 0))
    st = pl.BlockSpec((1, GLA_HEADS, GLA_DK, GLA_DV), lambda b, c: (b, 0, 0, 0))
    return pl.pallas_call(
        _gla_kernel,
        grid=(B, n),
        in_specs=[tok(GLA_QK), tok(GLA_QK), tok(GLA_W), tok(GLA_W), tok(GLA_QK),
                  pl.BlockSpec((1, GLA_DV), lambda b, c: (0, 0)), st],
        out_specs=[tok(GLA_W), st],
        out_shape=[jax.ShapeDtypeStruct((B, T, GLA_W), BF16),
                   jax.ShapeDtypeStruct((B, GLA_HEADS, GLA_DK, GLA_DV), F32)],
        scratch_shapes=[pltpu.VMEM((GLA_HEADS, GLA_DK, GLA_DV), F32)],
        compiler_params=_cparams(("arbitrary", "arbitrary")),
        name="gla",
    )(q, k, v, gg, lg, g_out.reshape(1, GLA_DV), s0)


def _t5_bucket_np(d):
    max_exact = REL_BUCKETS // 2
    d = np.maximum(d, 0)
    large = max_exact + (np.log(np.maximum(d, 1).astype(np.float32) / np.float32(max_exact))
                         / np.float32(math.log(REL_MAX_DIST / max_exact))
                         * np.float32(REL_BUCKETS - max_exact)).astype(np.int32)
    large = np.minimum(large, REL_BUCKETS - 1)
    return np.where(d < max_exact, d, large).astype(np.int32)


T5_FAR = 113
assert int(_t5_bucket_np(np.arange(T5_FAR, 1 << 16)).min()) == REL_BUCKETS - 1


def _t5_kernel(bias_ref, idx_ref, o_ref):
    idx = idx_ref[0]
    for h in range(DSA_HEADS):
        acc = jnp.zeros(idx.shape, F32)
        for bkt in range(REL_BUCKETS):
            acc = jnp.where(idx == bkt, bias_ref[bkt, h], acc)
        o_ref[0, h] = acc


def _t5_tiles(t5_bias, dist_tiles):
    idx = jnp.asarray(_t5_bucket_np(dist_tiles))
    n, r, c = idx.shape
    return pl.pallas_call(
        _t5_kernel,
        grid=(n,),
        in_specs=[pl.BlockSpec(memory_space=pltpu.SMEM),
                  pl.BlockSpec((1, r, c), lambda i: (i, 0, 0))],
        out_specs=pl.BlockSpec((1, DSA_HEADS, r, c), lambda i: (i, 0, 0, 0)),
        out_shape=jax.ShapeDtypeStruct((n, DSA_HEADS, r, c), F32),
        compiler_params=_cparams(("arbitrary",)),
        name="t5tiles",
    )(t5_bias, idx)


def _key_to_f32(m):
    bits = m ^ ((m >> 31) & jnp.int32(0x7FFFFFFF))
    return pltpu.bitcast(bits, F32)


KEY_NEG_INF = int(np.int32(-2139095041))
KEY_POS_INF = 0x7F800000


def _avg_floor(lo, hi):
    return (lo >> 1) + (hi >> 1) + (lo & hi & 1)


DSA_QB = 256
DSA_KV_STEP = 2048


def _dsa_prompt_kernel(qi_ref, wi_ref, kiT_ref, q_ref, k_ref, v_ref, tb_ref, far_ref, g_ref,
                       o_ref, sc_scr, thr_scr, m_scr, l_scr, acc_scr, *, topk, n_sub):
    i = pl.program_id(0)
    j = pl.program_id(1)
    QB = DSA_QB
    ntile = i + 1
    lrow = lax.broadcasted_iota(I32, (QB, QB), 0)
    lcol = lax.broadcasted_iota(I32, (QB, QB), 1)

    @pl.when(j == 0)
    def _scores_and_threshold():
        wi = wi_ref[...]
        wib = [jnp.broadcast_to(wi[:, h:h + 1], (QB, QB)) for h in range(IDX_HEADS)]

        def score_tile(t, carry):
            c0 = pl.multiple_of(t * QB, QB)
            kt = kiT_ref[:, pl.ds(c0, QB)]
            acc = jnp.zeros((QB, QB), F32)
            for h in range(IDX_HEADS):
                s = _dot(qi_ref[:, h * IDX_DH:(h + 1) * IDX_DH], kt)
                acc = acc + jnp.maximum(s, 0.0) * wib[h]
            acc = jnp.where((t < i) | (lcol <= lrow), acc, -jnp.inf)
            sc_scr[:, pl.ds(c0, QB)] = acc
            return carry

        lax.fori_loop(0, ntile, score_tile, 0)

        def count_gt(thr_b):
            def body(t, cnt):
                c0 = pl.multiple_of(t * QB, QB)
                s = sc_scr[:, pl.ds(c0, QB)]
                hit = jnp.where(s > thr_b, 1.0, 0.0)
                return cnt + hit[:, :LANE] + hit[:, LANE:]
            cnt = lax.fori_loop(0, ntile, body, jnp.zeros((QB, LANE), F32))
            return jnp.sum(cnt, axis=1, keepdims=True)

        def bis(_, lohi):
            lo, hi = lohi
            mid = _avg_floor(lo, hi)
            midf = _key_to_f32(mid)
            cnt = count_gt(jnp.concatenate([midf] * (QB // LANE), axis=1))
            ok = (cnt <= float(topk - 1)) & (mid > lo)
            return jnp.where(ok, lo, mid), jnp.where(ok, mid, hi)

        lo0 = jnp.full((QB, LANE), KEY_NEG_INF - 1, I32)
        hi0 = jnp.full((QB, LANE), KEY_POS_INF, I32)
        _, hi = lax.fori_loop(0, 32, bis, (lo0, hi0))
        thr_l = _key_to_f32(hi)
        thr = thr_l[:, :1]
        thr_b = jnp.concatenate([thr_l] * (QB // LANE), axis=1)
        thr_scr[...] = thr_l

        def count_ge():
            def body(t, cnt):
                c0 = pl.multiple_of(t * QB, QB)
                s = sc_scr[:, pl.ds(c0, QB)]
                hit = jnp.where(s >= thr_b, 1.0, 0.0)
                return cnt + hit[:, :LANE] + hit[:, LANE:]
            cnt = lax.fori_loop(0, ntile, body, jnp.zeros((QB, LANE), F32))
            return jnp.sum(cnt, axis=1, keepdims=True)

        n_ge = count_ge()
        finite = thr > -jnp.inf
        excess = jnp.where(finite & (n_ge > float(topk)), 1.0, 0.0)

        @pl.when(jnp.max(excess) > 0.0)
        def _ties():
            n_gt = count_gt(thr_b)
            need = float(topk) - n_gt

            def count_eq_upto(pos_b):
                def body(t, cnt):
                    c0 = pl.multiple_of(t * QB, QB)
                    s = sc_scr[:, pl.ds(c0, QB)]
                    hit = jnp.where((s == thr_b) & (lcol + c0 <= pos_b), 1.0, 0.0)
                    return cnt + hit[:, :LANE] + hit[:, LANE:]
                cnt = lax.fori_loop(0, ntile, body, jnp.zeros((QB, LANE), F32))
                return jnp.sum(cnt, axis=1, keepdims=True)

            def pbis(_, lohi):
                lo, hi = lohi
                mid = (lo + hi) >> 1
                ok = count_eq_upto(jnp.broadcast_to(mid, (QB, QB))) >= need
                return jnp.where(ok, lo, mid), jnp.where(ok, mid, hi)

            nbits = int(math.ceil(math.log2(sc_scr.shape[1]))) + 1
            plo = jnp.full((QB, 1), -1, I32)
            phi = jnp.full((QB, 1), sc_scr.shape[1] - 1, I32)
            _, pcut = lax.fori_loop(0, nbits, pbis, (plo, phi))
            pcut_b = jnp.broadcast_to(jnp.where(excess > 0.0, pcut, sc_scr.shape[1]), (QB, QB))

            def drop(t, carry):
                c0 = pl.multiple_of(t * QB, QB)
                s = sc_scr[:, pl.ds(c0, QB)]
                sc_scr[:, pl.ds(c0, QB)] = jnp.where((s == thr_b) & (lcol + c0 > pcut_b), -jnp.inf, s)
                return carry

            lax.fori_loop(0, ntile, drop, 0)

        m_scr[...] = jnp.full(m_scr.shape, NEG, F32)
        l_scr[...] = jnp.zeros(l_scr.shape, F32)
        acc_scr[...] = jnp.zeros(acc_scr.shape, F32)

    thr_b = jnp.concatenate([thr_scr[...]] * (QB // LANE), axis=1)

    def attend(sub, kind):
        t = j * n_sub + sub
        c0 = pl.multiple_of(t * QB, QB)
        s = sc_scr[:, pl.ds(c0, QB)]
        sel = s >= thr_b
        if kind == 2:
            sel = sel & (lcol <= lrow)
        r0 = sub * QB
        for h in range(DSA_HEADS):
            hs = slice(h * DSA_DH, (h + 1) * DSA_DH)
            logit = _dot_nt(q_ref[:, hs], k_ref[r0:r0 + QB, hs])
            if kind == 0:
                logit = logit + far_ref[h]
            else:
                logit = logit + tb_ref[kind - 1, h]
            logit = jnp.where(sel, logit, NEG)
            m_old = m_scr[h]
            m_new = jnp.maximum(m_old, jnp.max(logit, axis=1, keepdims=True))
            alpha = jnp.exp(m_old - m_new)
            p = jnp.exp(logit - m_new[:, :1])
            l_scr[h] = alpha * l_scr[h] + jnp.sum(p, axis=1, keepdims=True)
            acc_scr[:, hs] = alpha * acc_scr[:, hs] + _dot(p.astype(BF16), v_ref[r0:r0 + QB, hs])
            m_scr[h] = m_new

    for sub in range(n_sub):
        t = j * n_sub + sub

        @pl.when(t < i - 1)
        def _far():
            attend(sub, 0)

        @pl.when(t == i - 1)
        def _prev():
            attend(sub, 1)

        @pl.when(t == i)
        def _diag():
            attend(sub, 2)

    @pl.when(j == pl.num_programs(1) - 1)
    def _finish():
        outs = []
        for h in range(DSA_HEADS):
            hs = slice(h * DSA_DH, (h + 1) * DSA_DH)
            outs.append(acc_scr[:, hs] / l_scr[h])
        o = jnp.concatenate(outs, axis=1)
        o_ref[...] = _rms(o, g_ref[...]).astype(BF16)


def _dsa_prompt(qi, wi, kiT, q, k, v, t5_bias, g_dsa):
    T = q.shape[0]
    QB = DSA_QB
    topk = min(DSA_TOPK_MAX, T // 4)
    kv_step = min(DSA_KV_STEP, T)
    assert T % QB == 0 and T % kv_step == 0 and kv_step % QB == 0
    nq, nkv = T // QB, T // kv_step
    tq = np.arange(QB)[:, None]
    tk = np.arange(QB)[None, :]
    tb = _t5_tiles(t5_bias, np.stack([QB + tq - tk, tq - tk]))
    far = t5_bias[REL_BUCKETS - 1]
    last_kv = lambda i: ((i + 1) * QB - 1) // kv_step
    kv_spec = pl.BlockSpec((kv_step, DSA_W), lambda i, j: (jnp.minimum(j, last_kv(i)), 0))
    qrow = lambda w: pl.BlockSpec((QB, w), lambda i, j: (i, 0))
    return pl.pallas_call(
        functools.partial(_dsa_prompt_kernel, topk=topk, n_sub=kv_step // QB),
        grid=(nq, nkv),
        in_specs=[qrow(IDX_W), qrow(LANE), pl.BlockSpec((IDX_DH, T), lambda i, j: (0, 0)),
                  qrow(DSA_W), kv_spec, kv_spec,
                  pl.BlockSpec((2, DSA_HEADS, QB, QB), lambda i, j: (0, 0, 0, 0)),
                  pl.BlockSpec(memory_space=pltpu.SMEM),
                  pl.BlockSpec((1, DSA_W), lambda i, j: (0, 0))],
        out_specs=qrow(DSA_W),
        out_shape=jax.ShapeDtypeStruct((T, DSA_W), BF16),
        scratch_shapes=[pltpu.VMEM((QB, T), F32), pltpu.VMEM((QB, LANE), F32),
                        pltpu.VMEM((DSA_HEADS, QB, LANE), F32), pltpu.VMEM((DSA_HEADS, QB, LANE), F32),
                        pltpu.VMEM((QB, DSA_W), F32)],
        compiler_params=_cparams(("arbitrary", "arbitrary")),
        name="dsa_p",
    )(qi, wi, kiT, q, k, v, tb, far, g_dsa.reshape(1, DSA_W))


DSA_S_ROWS = 8
DSA_S_GROUP = 8


def _dsa_sample_kernel(pt_ref, qi_ref, wi_ref, kin_ref, q_ref, kn_ref, vn_ref, tb_ref, far_ref, g_ref,
                       ckidx_hbm, ck_hbm, cv_hbm, o_ref,
                       kidx_buf, sc_scr, kbuf, vbuf, m_scr, l_scr, acc_scr, sem_i, sem_k, sem_v,
                       *, topk, n_new):
    b = pl.program_id(0)
    n_pages = pt_ref.shape[1]
    PS = kidx_buf.shape[0] // n_pages
    past = n_pages * PS
    R = DSA_S_ROWS
    G = DSA_S_GROUP
    GK = G * PS
    n_groups = n_pages // G
    L = past + LANE

    def kidx_copy(p):
        return pltpu.make_async_copy(ckidx_hbm.at[pt_ref[b, p]], kidx_buf.at[pl.ds(p * PS, PS)], sem_i)

    def kv_copies(g, slot, p):
        page = pt_ref[b, g * G + p]
        return (pltpu.make_async_copy(ck_hbm.at[page], kbuf.at[slot, pl.ds(p * PS, PS)], sem_k.at[slot]),
                pltpu.make_async_copy(cv_hbm.at[page], vbuf.at[slot, pl.ds(p * PS, PS)], sem_v.at[slot]))

    def start_group(g, slot):
        def body(p, c):
            ck, cv = kv_copies(g, slot, p)
            ck.start()
            cv.start()
            return c
        lax.fori_loop(0, G, body, 0)

    def wait_group(g, slot):
        def body(p, c):
            ck, cv = kv_copies(g, slot, p)
            ck.wait()
            cv.wait()
            return c
        lax.fori_loop(0, G, body, 0)

    def istart(p, c):
        kidx_copy(p).start()
        return c

    def iwait(p, c):
        kidx_copy(p).wait()
        return c

    lax.fori_loop(0, n_pages, istart, 0)
    start_group(0, 0)
    lax.fori_loop(0, n_pages, iwait, 0)

    qi = qi_ref[0]
    wi = wi_ref[0]

    def head_sum(s):
        r = jnp.maximum(s, 0.0) * wi
        tot = r[0:R]
        for h in range(1, IDX_HEADS):
            tot = tot + r[h * R:(h + 1) * R]
        return tot

    def score_chunk(cidx, c):
        c0 = pl.multiple_of(cidx * GK, GK)
        kc = kidx_buf[pl.ds(c0, GK), :].astype(BF16)
        sc_scr[:, pl.ds(c0, GK)] = head_sum(_dot_nt(qi, kc))
        return c

    lax.fori_loop(0, n_groups, score_chunk, 0)
    trow = lax.broadcasted_iota(I32, (R, LANE), 0)
    tcol = lax.broadcasted_iota(I32, (R, LANE), 1)
    new_ok = (tcol <= trow) & (tcol < n_new)
    sc_scr[:, past:L] = jnp.where(new_ok, head_sum(_dot_nt(qi, kin_ref[0])), -jnp.inf)

    def count(pred):
        return jnp.sum(jnp.where(pred, 1.0, 0.0), axis=1, keepdims=True)

    def bis(_, lohi):
        lo, hi = lohi
        mid = _avg_floor(lo, hi)
        ok = (count(sc_scr[...] > _key_to_f32(mid)[:, :1]) <= float(topk - 1)) & (mid > lo)
        return jnp.where(ok, lo, mid), jnp.where(ok, mid, hi)

    lo0 = jnp.full((R, LANE), KEY_NEG_INF - 1, I32)
    hi0 = jnp.full((R, LANE), KEY_POS_INF, I32)
    _, hi = lax.fori_loop(0, 32, bis, (lo0, hi0))
    thr = _key_to_f32(hi)[:, :1]
    excess = jnp.where((thr > -jnp.inf) & (count(sc_scr[...] >= thr) > float(topk)), 1.0, 0.0)

    @pl.when(jnp.max(excess) > 0.0)
    def _ties():
        need = float(topk) - count(sc_scr[...] > thr)
        pos = lax.broadcasted_iota(I32, (R, L), 1)

        def pbis(_, lohi):
            lo, hi = lohi
            mid = (lo + hi) >> 1
            ok = count((sc_scr[...] == thr) & (pos <= mid)) >= need
            return jnp.where(ok, lo, mid), jnp.where(ok, mid, hi)

        nbits = int(math.ceil(math.log2(L))) + 1
        _, pcut = lax.fori_loop(0, nbits, pbis, (jnp.full((R, 1), -1, I32), jnp.full((R, 1), L - 1, I32)))
        pcut = jnp.where(excess > 0.0, pcut, L)
        s = sc_scr[...]
        sc_scr[...] = jnp.where((s == thr) & (pos > pcut), -jnp.inf, s)

    m_scr[...] = jnp.full(m_scr.shape, NEG, F32)
    l_scr[...] = jnp.zeros(l_scr.shape, F32)
    acc_scr[...] = jnp.zeros(acc_scr.shape, F32)
    q = q_ref[0]

    def softmax_step(h, logit, sel, vh):
        hs = slice(h * DSA_DH, (h + 1) * DSA_DH)
        logit = jnp.where(sel, logit, NEG)
        m_old = m_scr[h]
        m_new = jnp.maximum(m_old, jnp.max(logit, axis=1, keepdims=True))
        alpha = jnp.exp(m_old - m_new)
        p = jnp.exp(logit - m_new[:, :1])
        l_scr[h] = alpha * l_scr[h] + jnp.sum(p, axis=1, keepdims=True)
        acc_scr[:, hs] = alpha * acc_scr[:, hs] + _dot(p.astype(BF16), vh)
        m_scr[h] = m_new

    def attend_group(g, slot, last):
        c0 = pl.multiple_of(g * GK, GK)
        sel = sc_scr[:, pl.ds(c0, GK)] >= thr
        kg = kbuf[slot].astype(BF16)
        vg = vbuf[slot].astype(BF16)
        for h in range(DSA_HEADS):
            hs = slice(h * DSA_DH, (h + 1) * DSA_DH)
            logit = _dot_nt(q[:, hs], kg[:, hs])
            if last:
                bias = jnp.concatenate([jnp.full((R, GK - PS), far_ref[h], F32), tb_ref[0, h]], axis=1)
                logit = logit + bias
            else:
                logit = logit + far_ref[h]
            softmax_step(h, logit, sel, vg[:, hs])

    def group_body(g, c):
        slot = g % 2
        wait_group(g, slot)

        @pl.when(g + 1 < n_groups)
        def _():
            start_group(g + 1, 1 - slot)

        @pl.when(g < n_groups - 1)
        def _():
            attend_group(g, slot, False)

        @pl.when(g == n_groups - 1)
        def _():
            attend_group(g, slot, True)
        return c

    lax.fori_loop(0, n_groups, group_body, 0)

    sel_new = (sc_scr[:, past:L] >= thr) & new_ok
    kn = kn_ref[0]
    vn = vn_ref[0]
    for h in range(DSA_HEADS):
        hs = slice(h * DSA_DH, (h + 1) * DSA_DH)
        softmax_step(h, _dot_nt(q[:, hs], kn[:, hs]) + tb_ref[1, h], sel_new, vn[:, hs])

    outs = []
    for h in range(DSA_HEADS):
        hs = slice(h * DSA_DH, (h + 1) * DSA_DH)
        outs.append(acc_scr[:, hs] / l_scr[h])
    o_ref[0] = _rms(jnp.concatenate(outs, axis=1), g_ref[...]).astype(BF16)


def _dsa_sample(page_table, qi, wi, ki_new, q, k_new, v_new, cache_kidx, cache_k, cache_v, t5_bias, g_dsa, n_new):
    Bd, n_pages = page_table.shape
    n_phys, PS = cache_kidx.shape[:2]
    past = n_pages * PS
    R = DSA_S_ROWS
    topk = min(DSA_TOPK_MAX, (past + n_new) // 4)
    assert n_pages % DSA_S_GROUP == 0 and PS == LANE and n_new <= R
    t = np.arange(R)[:, None]
    c = np.arange(LANE)[None, :]
    tb = _t5_tiles(t5_bias, np.stack([PS + t - c, t - c]))
    far = t5_bias[REL_BUCKETS - 1]
    GK = DSA_S_GROUP * PS
    per_b = lambda shp: pl.BlockSpec((1,) + shp, lambda b, pt: (b,) + (0,) * len(shp))
    grid_spec = pltpu.PrefetchScalarGridSpec(
        num_scalar_prefetch=1,
        grid=(Bd,),
        in_specs=[per_b((IDX_HEADS * R, IDX_DH)), per_b((IDX_HEADS * R, 1)), per_b((LANE, IDX_DH)),
                  per_b((R, DSA_W)), per_b((LANE, DSA_W)), per_b((LANE, DSA_W)),
                  pl.BlockSpec((2, DSA_HEADS, R, LANE), lambda b, pt: (0, 0, 0, 0)),
                  pl.BlockSpec(memory_space=pltpu.SMEM),
                  pl.BlockSpec((1, DSA_W), lambda b, pt: (0, 0)),
                  pl.BlockSpec(memory_space=pl.ANY), pl.BlockSpec(memory_space=pl.ANY),
                  pl.BlockSpec(memory_space=pl.ANY)],
        out_specs=per_b((R, DSA_W)),
        scratch_shapes=[pltpu.VMEM((past, IDX_DH), F32), pltpu.VMEM((R, past + LANE), F32),
                        pltpu.VMEM((2, GK, DSA_W), F32), pltpu.VMEM((2, GK, DSA_W), F32),
                        pltpu.VMEM((DSA_HEADS, R, LANE), F32), pltpu.VMEM((DSA_HEADS, R, LANE), F32),
                        pltpu.VMEM((R, DSA_W), F32),
                        pltpu.SemaphoreType.DMA(()), pltpu.SemaphoreType.DMA((2,)),
                        pltpu.SemaphoreType.DMA((2,))],
    )
    return pl.pallas_call(
        functools.partial(_dsa_sample_kernel, topk=topk, n_new=n_new),
        grid_spec=grid_spec,
        out_shape=jax.ShapeDtypeStruct((Bd, R, DSA_W), BF16),
        compiler_params=_cparams(("arbitrary",)),
        name="dsa_s",
    )(page_table, qi, wi, ki_new, q, k_new, v_new, tb, far, g_dsa.reshape(1, DSA_W),
      cache_kidx, cache_k.reshape(n_phys, PS, DSA_W), cache_v.reshape(n_phys, PS, DSA_W))


def _post_kernel(og_ref, od_ref, x_ref, gt_ref, g2_ref, sc_ref, sh_ref, wo_ref, wpq_ref,
                 x1_ref, h2_ref, pq_ref):
    half = og_ref.shape[1]
    y = _dot(og_ref[...], wo_ref[:half, :]) + _dot(od_ref[...], wo_ref[half:, :])
    x1 = x_ref[...] + gt_ref[...] * y
    x1_ref[...] = x1
    h2 = (_rms(x1, g2_ref[...]) * (1.0 + sc_ref[...]) + sh_ref[...]).astype(BF16)
    h2_ref[...] = h2
    pq_ref[...] = _dot(h2, wpq_ref[...]).astype(BF16)


def _post(o_gla, o_dsa, x, gt1, g2, sc2, sh2, w_out, w_pq, tm):
    r, d = x.shape
    mod_spec = (pl.BlockSpec((1, d), lambda i: (0, 0)) if gt1.shape[0] == 1
                else pl.BlockSpec((tm, d), lambda i: (i, 0)))
    full = lambda shp: pl.BlockSpec(shp, lambda i: (0,) * len(shp))
    row = lambda w: pl.BlockSpec((tm, w), lambda i: (i, 0))
    return pl.pallas_call(
        _post_kernel,
        grid=(r // tm,),
        in_specs=[row(o_gla.shape[1]), row(o_dsa.shape[1]), row(d), mod_spec, full((1, d)), mod_spec, mod_spec,
                  full(w_out.shape), full(w_pq.shape)],
        out_specs=[row(d), row(d), row(w_pq.shape[1])],
        out_shape=[jax.ShapeDtypeStruct((r, d), F32), jax.ShapeDtypeStruct((r, d), BF16),
                   jax.ShapeDtypeStruct((r, w_pq.shape[1]), BF16)],
        compiler_params=_cparams(("arbitrary",)),
        name="post",
    )(o_gla, o_dsa, x, gt1, g2.reshape(1, d), sc2, sh2, w_out, w_pq)


PEER_NCAND = PEER_TOPK + 1
SUBLANE = 8


def _top_rows(work, n):
    vals = []
    for r in range(n):
        m = jnp.max(work, axis=0, keepdims=True)
        vals.append(m)
        if r + 1 < n:
            work = jnp.where(work == m, -jnp.inf, work)
    return vals


def _peer_sel_kernel(pq_ref, k1_ref, k2_ref, e1_ref, e2_ref, s2_ref, c_ref):
    tn = pq_ref.shape[0]
    n = PEER_NCAND
    npad = -(-n // SUBLANE) * SUBLANE
    for h in range(PEER_HEADS):
        pqh = pq_ref[:, h * PEER_DKEY:(h + 1) * PEER_DKEY]
        s1 = _dot_nt(k1_ref[h], pqh)
        s2 = _dot_nt(k2_ref[h], pqh)
        v1 = _top_rows(s1, n)
        v2 = _top_rows(s2, n)
        v2a = jnp.concatenate(v2 + [jnp.full((npad - n, tn), -jnp.inf, F32)], axis=0)
        cands = [v1[0] + v2a] + [v1[a] + v2a[:SUBLANE] for a in range(1, n)]
        assert all(n // (a + 1) <= SUBLANE for a in range(1, n))
        c = _top_rows(jnp.concatenate(cands, axis=0), n)
        thr = 0.5 * (c[PEER_TOPK - 1] + c[PEER_TOPK])
        z = jnp.ones_like(c[0])
        for r in range(1, PEER_TOPK):
            z = z + jnp.exp(c[r] - c[0])
        e1_ref[h] = jnp.exp(s1 - v1[0])
        e2_ref[h] = jnp.exp(s2 - v2[0]) / z
        s2_ref[h] = s2
        c_ref[h] = thr - s1


def _peer_sel(pq, k1p, k2p, tn):
    n, w = pq.shape
    blk = pl.BlockSpec((PEER_HEADS, PEER_NKEYS, tn), lambda i: (0, 0, i))
    full = lambda shp: pl.BlockSpec(shp, lambda i: (0,) * len(shp))
    shp = jax.ShapeDtypeStruct((PEER_HEADS, PEER_NKEYS, n), F32)
    return pl.pallas_call(
        _peer_sel_kernel,
        grid=(n // tn,),
        in_specs=[pl.BlockSpec((tn, w), lambda i: (i, 0)), full(k1p.shape), full(k2p.shape)],
        out_specs=[blk] * 4,
        out_shape=[shp] * 4,
        compiler_params=_cparams(("arbitrary",)),
        name="peer_sel",
    )(pq, k1p, k2p)


PEER_TN = 512
PEER_ROWS = 4


def _peer_kernel(h2_ref, e1_ref, e2_ref, s2_ref, c_ref, u_ref, vt_ref, x1_ref, gt_ref, gf_ref,
                 y_ref, acc_scr, *, final):
    j = pl.program_id(1)

    @pl.when(j == 0)
    def _():
        acc_scr[...] = jnp.zeros(acc_scr.shape, F32)

    aT = _dot_nt(u_ref[...], h2_ref[...])
    w_rows = []
    for r in range(PEER_ROWS):
        i = j * PEER_ROWS + r
        gate = jnp.zeros((PEER_NKEYS, aT.shape[1]), F32)
        for h in range(PEER_HEADS):
            e1row = e1_ref[h, pl.ds(i, 1), :]
            crow = c_ref[h, pl.ds(i, 1), :]
            gate = gate + e1row * jnp.where(s2_ref[h] >= crow, e2_ref[h], 0.0)
        a = aT[r * PEER_NKEYS:(r + 1) * PEER_NKEYS]
        gelu = 0.5 * a * (1.0 + lax.erf(a * (2.0 ** -0.5)))
        w_rows.append((gate * gelu).astype(BF16))
    wT = jnp.concatenate(w_rows, axis=0)
    acc_scr[...] += _dot(vt_ref[...], wT)

    @pl.when(j == pl.num_programs(1) - 1)
    def _():
        x2 = x1_ref[...] + gt_ref[...] * acc_scr[...].T
        y_ref[...] = _rms(x2, gf_ref[...]) if final else x2


def _peer(h2, e1, e2, s2, c, u_b, vt_b, x1, gt2, g_final, final):
    n, d = x1.shape
    tn = PEER_TN
    te = PEER_ROWS * PEER_NKEYS
    n_exp = u_b.shape[0]
    mod_spec = (pl.BlockSpec((1, d), lambda i, j: (0, 0)) if gt2.shape[0] == 1
                else pl.BlockSpec((tn, d), lambda i, j: (i, 0)))
    sel = pl.BlockSpec((PEER_HEADS, PEER_NKEYS, tn), lambda i, j: (0, 0, i))
    row = pl.BlockSpec((tn, d), lambda i, j: (i, 0))
    return pl.pallas_call(
        functools.partial(_peer_kernel, final=final),
        grid=(n // tn, n_exp // te),
        in_specs=[row, sel, sel, sel, sel,
                  pl.BlockSpec((te, d), lambda i, j: (j, 0)),
                  pl.BlockSpec((d, te), lambda i, j: (0, j)),
                  row, mod_spec, pl.BlockSpec((1, d), lambda i, j: (0, 0))],
        out_specs=row,
        out_shape=jax.ShapeDtypeStruct((n, d), F32),
        scratch_shapes=[pltpu.VMEM((d, tn), F32)],
        compiler_params=_cparams(("arbitrary", "arbitrary")),
        name="peer",
    )(h2, e1, e2, s2, c, u_b, vt_b, x1, gt2, g_final.reshape(1, d))


def _pad_in_proj(w_in):
    parts, o = [], 0
    for s, ps in zip(SPLIT_SIZES, _PAD_SIZES):
        parts.append(jnp.pad(w_in[:, o:o + s], ((0, 0), (0, ps - s))))
        o += s
    return jnp.concatenate(parts, axis=1).astype(BF16)


def _pad_peer_keys(keys, lo):
    half = keys.shape[-1]
    return jnp.pad(keys, ((0, 0), (0, 0), (lo, PEER_DKEY - half - lo))).astype(BF16)


def _pad_axis(a, axis, size):
    pad = [(0, 0)] * a.ndim
    pad[axis] = (0, size - a.shape[axis])
    return jnp.pad(a, pad)


def _rows_mod(mod, reps):
    parts = jnp.split(mod, 6, axis=-1)
    if mod.shape[0] == 1:
        return parts
    return [jnp.repeat(p, reps, axis=0) for p in parts]


def kernel(x_prompt, x_sample, cache_k, cache_v, cache_kidx, state_gla, page_table, c_prompt, c_sample,
           w_ada, b_ada, g_norm1, g_norm2, w_in, w_gate_a2, b_gate_a2, g_gla_out, g_dsa_out, w_out,
           t5_bias, w_peer_q, peer_keys1, peer_keys2, peer_u, peer_v, g_final):
    depth = w_ada.shape[0]
    B, T, D = x_prompt.shape
    Bd, Ts, _ = x_sample.shape
    xp = x_prompt.reshape(B * T, D)
    xs = x_sample.reshape(Bd * Ts, D)
    R = DSA_S_ROWS
    outs = [[] for _ in range(8)]
    for l in range(depth):
        w_in_p = _pad_in_proj(w_in[l])
        wa2_p = _pad_axis(w_gate_a2[l], 0, LANE).astype(BF16)
        w_out_b = w_out[l].astype(BF16)
        w_pq_b = w_peer_q[l].astype(BF16)
        k1p = _pad_peer_keys(peer_keys1[l], 0)
        k2p = _pad_peer_keys(peer_keys2[l], PEER_DKEY // 2)
        u_b = peer_u[l].astype(BF16)
        vt_b = peer_v[l].T.astype(BF16)
        last = l == depth - 1

        c_all = jnp.concatenate([c_prompt, c_sample], axis=0)
        c_all = _pad_axis(c_all, 0, -(-(B + Bd) // SUBLANE) * SUBLANE)
        mod = _ada(c_all, w_ada[l], b_ada[l])
        mod_p = _rows_mod(mod[:B], T)
        mod_s = _rows_mod(mod[B:B + Bd], Ts)

        sh1, sc1, gt1, sh2, sc2, gt2 = mod_p
        (gq, gk, gv, gg, lg, dq, dk, dv, dkb, dvb, qi, ki, kib, wi) = _proj(
            xp, g_norm1[l], sc1, sh1, w_in_p, wa2_p, b_gate_a2[l], 256)
        s0p = jnp.zeros((B, GLA_HEADS, GLA_DK, GLA_DV), F32)
        r3 = lambda a: a.reshape(B, T, a.shape[-1])
        o_gla, st_p = _gla(r3(gq), r3(gk), r3(gv), r3(gg), r3(lg), g_gla_out[l], s0p, min(64, T))
        o_dsa = jnp.concatenate([
            _dsa_prompt(qi[b * T:(b + 1) * T], wi[b * T:(b + 1) * T], kib[b * T:(b + 1) * T].T,
                        dq[b * T:(b + 1) * T], dkb[b * T:(b + 1) * T], dvb[b * T:(b + 1) * T],
                        t5_bias, g_dsa_out[l]) for b in range(B)], axis=0)
        x1, h2, pq = _post(o_gla.reshape(B * T, GLA_W), o_dsa, xp, gt1, g_norm2[l], sc2, sh2, w_out_b, w_pq_b, 256)
        e1, e2, s2, cth = _peer_sel(pq, k1p, k2p, 256)
        xp = _peer(h2, e1, e2, s2, cth, u_b, vt_b, x1, gt2, g_final, last)
        outs[0].append(dk.reshape(B, T, DSA_HEADS, DSA_DH))
        outs[1].append(dv.reshape(B, T, DSA_HEADS, DSA_DH))
        outs[2].append(ki.reshape(B, T, IDX_DH))
        outs[3].append(st_p)

        sh1, sc1, gt1, sh2, sc2, gt2 = mod_s
        (gq, gk, gv, gg, lg, dq, dk, dv, dkb, dvb, qi, ki, kib, wi) = _proj(
            xs, g_norm1[l], sc1, sh1, w_in_p, wa2_p, b_gate_a2[l], 256)
        tokpad = lambda a, n: _pad_axis(a.reshape(Bd, Ts, a.shape[-1]), 1, n)
        o_gla, st_s = _gla(tokpad(gq, R), tokpad(gk, R), tokpad(gv, R), tokpad(gg, R), tokpad(lg, R),
                           g_gla_out[l], state_gla[l], R)
        qi_s = _pad_axis(qi.reshape(Bd, Ts, IDX_HEADS, IDX_DH).transpose(0, 2, 1, 3), 2, R)
        wi_s = _pad_axis(wi[:, :IDX_HEADS].reshape(Bd, Ts, IDX_HEADS).transpose(0, 2, 1), 2, R)
        o_dsa = _dsa_sample(page_table, qi_s.reshape(Bd, IDX_HEADS * R, IDX_DH),
                            wi_s.reshape(Bd, IDX_HEADS * R, 1), tokpad(kib, LANE), tokpad(dq, R),
                            tokpad(dkb, LANE), tokpad(dvb, LANE),
                            cache_kidx[l], cache_k[l], cache_v[l], t5_bias, g_dsa_out[l], Ts)
        x1, h2, pq = _post(o_gla[:, :Ts].reshape(Bd * Ts, GLA_W), o_dsa[:, :Ts].reshape(Bd * Ts, DSA_W),
                           xs, gt1, g_norm2[l], sc2, sh2, w_out_b, w_pq_b, 256)
        e1, e2, s2, cth = _peer_sel(pq, k1p, k2p, 256)
        xs = _peer(h2, e1, e2, s2, cth, u_b, vt_b, x1, gt2, g_final, last)
        outs[4].append(dk.reshape(Bd, Ts, DSA_HEADS, DSA_DH))
        outs[5].append(dv.reshape(Bd, Ts, DSA_HEADS, DSA_DH))
        outs[6].append(ki.reshape(Bd, Ts, IDX_DH))
        outs[7].append(st_s)

    return (xp.reshape(B, T, D), xs.reshape(Bd, Ts, D)) + tuple(jnp.stack(o) for o in outs)
```

```python
import functools
import math

import numpy as np
import jax
import jax.numpy as jnp
from jax import lax
from jax.experimental import pallas as pl
from jax.experimental.pallas import tpu as pltpu

F32 = jnp.float32
BF16 = jnp.bfloat16
I32 = jnp.int32

EPS = 1e-6
GLA_HEADS, GLA_DK, GLA_DV = 4, 64, 128
GLA_GATE_RANK = 16
GLA_GATE_NORM = 16.0
GLA_QK = GLA_HEADS * GLA_DK
GLA_W = GLA_HEADS * GLA_DV
DSA_HEADS, DSA_DH = 4, 128
DSA_W = DSA_HEADS * DSA_DH
IDX_HEADS, IDX_DH = 8, 64
IDX_W = IDX_HEADS * IDX_DH
DSA_TOPK_MAX = 256
REL_BUCKETS, REL_MAX_DIST = 32, 128
PEER_HEADS, PEER_NKEYS, PEER_DKEY, PEER_TOPK = 8, 128, 128, 16
SPLIT_SIZES = (GLA_QK, GLA_QK, GLA_W, GLA_W, GLA_GATE_RANK,
               DSA_W, DSA_W, DSA_W, IDX_W, IDX_DH, IDX_HEADS)

LANE = 128
SUBLANE = 8
NEG = -1e30
VMEM_LIMIT = 56 * 1024 * 1024

_PAD_SIZES = tuple(-(-s // LANE) * LANE for s in SPLIT_SIZES)
_PAD_OFFS = tuple(int(v) for v in np.cumsum((0,) + _PAD_SIZES))
(O_GQ, O_GK, O_GV, O_GG, O_GA, O_DQ, O_DK, O_DV, O_QI, O_KI, O_WI, D_IN_PAD) = _PAD_OFFS


def _cparams(sem):
    return pltpu.CompilerParams(dimension_semantics=sem, vmem_limit_bytes=VMEM_LIMIT)


def _dot(a, b):
    return jnp.dot(a, b, preferred_element_type=F32)


def _dot_nt(a, b):
    return lax.dot_general(a, b, (((1,), (1,)), ((), ())), preferred_element_type=F32)


def _rms(x, g):
    return x * lax.rsqrt(jnp.mean(x * x, axis=-1, keepdims=True) + EPS) * g


def _ada_kernel(c_ref, w_ref, b_ref, o_ref):
    c = c_ref[...]
    s = c * jax.nn.sigmoid(c)
    o_ref[...] = _dot(s.astype(BF16), w_ref[...].astype(BF16)) + b_ref[...]


def _ada(c, w, b):
    m, d = c.shape
    n = w.shape[1]
    tn = 1024
    return pl.pallas_call(
        _ada_kernel,
        grid=(n // tn,),
        in_specs=[pl.BlockSpec((m, d), lambda j: (0, 0)),
                  pl.BlockSpec((d, tn), lambda j: (0, j)),
                  pl.BlockSpec((1, tn), lambda j: (0, j))],
        out_specs=pl.BlockSpec((m, tn), lambda j: (0, j)),
        out_shape=jax.ShapeDtypeStruct((m, n), F32),
        compiler_params=_cparams(("arbitrary",)),
        name="ada",
    )(c, w, b.reshape(1, n))


def _proj_kernel(x_ref, g_ref, sc_ref, sh_ref, w_ref, wa2_ref, ba2_ref,
                 q_ref, k_ref, v_ref, gg_ref, lg_ref, dq_ref, dk_ref, dv_ref,
                 dkb_ref, dvb_ref, qi_ref, ki_ref, kib_ref, wi_ref):
    x = x_ref[...]
    h = _rms(x, g_ref[...]) * (1.0 + sc_ref[...]) + sh_ref[...]
    p = _dot(h.astype(BF16), w_ref[...])
    q_ref[...] = p[:, O_GQ:O_GK] * (GLA_DK ** -0.5)
    k_ref[...] = p[:, O_GK:O_GV]
    v_ref[...] = p[:, O_GV:O_GG]
    gg_ref[...] = p[:, O_GG:O_GA]
    z = _dot(p[:, O_GA:O_DQ].astype(BF16), wa2_ref[...]) + ba2_ref[...]
    lg_ref[...] = (jnp.minimum(z, 0.0) - jnp.log(1.0 + jnp.exp(-jnp.abs(z)))) * (1.0 / GLA_GATE_NORM)
    dq_ref[...] = (p[:, O_DQ:O_DK] * (DSA_DH ** -0.5)).astype(BF16)
    dk = p[:, O_DK:O_DV]
    dv = p[:, O_DV:O_QI]
    dk_ref[...] = dk
    dv_ref[...] = dv
    dkb_ref[...] = dk.astype(BF16)
    dvb_ref[...] = dv.astype(BF16)
    qi_ref[...] = (p[:, O_QI:O_KI] * (IDX_DH ** -0.5)).astype(BF16)
    ki = p[:, O_KI:O_KI + IDX_DH]
    ki_ref[...] = ki
    kib_ref[...] = ki.astype(BF16)
    wi_ref[...] = p[:, O_WI:D_IN_PAD] * (IDX_HEADS ** -0.5)


def _proj(x, g1, sc, sh, w_in_p, wa2_p, ba2, tm):
    r, d = x.shape
    rows_mod = sc.shape[0]
    mod_spec = (pl.BlockSpec((1, d), lambda i: (0, 0)) if rows_mod == 1
                else pl.BlockSpec((tm, d), lambda i: (i, 0)))
    full = lambda shp: pl.BlockSpec(shp, lambda i: (0,) * len(shp))
    widths = [(GLA_QK, F32), (GLA_QK, F32), (GLA_W, F32), (GLA_W, F32), (GLA_QK, F32),
              (DSA_W, BF16), (DSA_W, F32), (DSA_W, F32), (DSA_W, BF16), (DSA_W, BF16),
              (IDX_W, BF16), (IDX_DH, F32), (IDX_DH, BF16), (LANE, F32)]
    return pl.pallas_call(
        _proj_kernel,
        grid=(r // tm,),
        in_specs=[pl.BlockSpec((tm, d), lambda i: (i, 0)), full((1, d)), mod_spec, mod_spec,
                  full(w_in_p.shape), full(wa2_p.shape), full((1, GLA_QK))],
        out_specs=[pl.BlockSpec((tm, w), lambda i: (i, 0)) for w, _ in widths],
        out_shape=[jax.ShapeDtypeStruct((r, w), dt) for w, dt in widths],
        compiler_params=_cparams(("arbitrary",)),
        name="proj",
    )(x, g1.reshape(1, d), sc, sh, w_in_p, wa2_p, ba2.reshape(1, GLA_QK))


GLA_SUB = 16


def _gla_kernel(q_ref, k_ref, v_ref, gg_ref, lg_ref, go_ref, s0_ref, o_ref, s_ref, st_scr):
    c = pl.program_id(1)

    @pl.when(c == 0)
    def _():
        st_scr[...] = s0_ref[0]

    q = q_ref[0]
    k = k_ref[0]
    v = v_ref[0]
    lg = lg_ref[0]
    C = q.shape[0]
    rows = lax.broadcasted_iota(I32, (C, C), 0)
    cols = lax.broadcasted_iota(I32, (C, C), 1)
    tri = (rows >= cols).astype(F32)
    b = jnp.dot(tri, lg, preferred_element_type=F32, precision=lax.Precision.HIGHEST)
    b_last = b[C - 1:C, :]
    qc = q * jnp.exp(b)
    kc = k * jnp.exp(b_last - b)
    kcT = kc.T
    dec_col = jnp.broadcast_to(jnp.exp(jnp.sum(lg.T, axis=1, keepdims=True)), (GLA_QK, GLA_DV))
    go = go_ref[...]
    outs = []
    for h in range(GLA_HEADS):
        sl = slice(h * GLA_DK, (h + 1) * GLA_DK)
        vh = v[:, h * GLA_DV:(h + 1) * GLA_DV].astype(BF16)
        S = st_scr[h]
        o = _dot(qc[:, sl].astype(BF16), S.astype(BF16))
        o_rows = []
        sub = min(GLA_SUB, C)
        for blk in range(C // sub):
            r0, r1 = blk * sub, (blk + 1) * sub
            bref = b[r0 - 1:r0, sl] if blk > 0 else jnp.zeros((1, GLA_DK), F32)
            qt = q[r0:r1, sl] * jnp.exp(b[r0:r1, sl] - bref)
            kt = k[:r1, sl] * jnp.exp(bref - b[:r1, sl])
            a = _dot_nt(qt.astype(BF16), kt.astype(BF16))
            causal = (lax.broadcasted_iota(I32, (sub, r1), 1)
                      <= lax.broadcasted_iota(I32, (sub, r1), 0) + r0)
            a = jnp.where(causal, a, 0.0)
            o_rows.append(_dot(a.astype(BF16), vh[:r1]))
        o = o + (jnp.concatenate(o_rows, axis=0) if len(o_rows) > 1 else o_rows[0])
        st_scr[h] = dec_col[sl] * S + _dot(kcT[sl].astype(BF16), vh)
        outs.append(_rms(o, go))
    o_all = jnp.concatenate(outs, axis=1)
    gg = gg_ref[0]
    o_ref[0] = (o_all * (gg * jax.nn.sigmoid(gg))).astype(BF16)

    @pl.when(c == pl.num_programs(1) - 1)
    def _():
        s_ref[0] = st_scr[...]


def _gla(q, k, v, gg, lg, g_out, s0, chunk):
    B, T, _ = q.shape
    n = T // chunk
    tok = lambda w: pl.BlockSpec((1, chunk, w), lambda b, c: (b, c, 0))
    st = pl.BlockSpec((1, GLA_HEADS, GLA_DK, GLA_DV), lambda b, c: (b, 0, 0, 0))
    return pl.pallas_call(
        _gla_kernel,
        grid=(B, n),
        in_specs=[tok(GLA_QK), tok(GLA_QK), tok(GLA_W), tok(GLA_W), tok(GLA_QK),
                  pl.BlockSpec((1, GLA_DV), lambda b, c: (0, 0)), st],
        out_specs=[tok(GLA_W), st],
        out_shape=[jax.ShapeDtypeStruct((B, T, GLA_W), BF16),
                   jax.ShapeDtypeStruct((B, GLA_HEADS, GLA_DK, GLA_DV), F32)],
        scratch_shapes=[pltpu.VMEM((GLA_HEADS, GLA_DK, GLA_DV), F32)],
        compiler_params=_cparams(("arbitrary", "arbitrary")),
        name="gla",
    )(q, k, v, gg, lg, g_out.reshape(1, GLA_DV), s0)


def _t5_bucket_np(d):
    max_exact = REL_BUCKETS // 2
    d = np.maximum(d, 0)
    large = max_exact + (np.log(np.maximum(d, 1).astype(np.float32) / np.float32(max_exact))
                         / np.float32(math.log(REL_MAX_DIST / max_exact))
                         * np.float32(REL_BUCKETS - max_exact)).astype(np.int32)
    large = np.minimum(large, REL_BUCKETS - 1)
    return np.where(d < max_exact, d, large).astype(np.int32)


T5_FAR = 113
assert int(_t5_bucket_np(np.arange(T5_FAR, 1 << 16)).min()) == REL_BUCKETS - 1


def _t5_kernel(bias_ref, idx_ref, o_ref):
    idx = idx_ref[0]
    for h in range(DSA_HEADS):
        acc = jnp.zeros(idx.shape, F32)
        for bkt in range(REL_BUCKETS):
            acc = jnp.where(idx == bkt, bias_ref[bkt, h], acc)
        o_ref[0, h] = acc


def _t5_tiles(t5_bias, dist_tiles):
    idx = jnp.asarray(_t5_bucket_np(dist_tiles))
    n, r, c = idx.shape
    return pl.pallas_call(
        _t5_kernel,
        grid=(n,),
        in_specs=[pl.BlockSpec(memory_space=pltpu.SMEM),
                  pl.BlockSpec((1, r, c), lambda i: (i, 0, 0))],
        out_specs=pl.BlockSpec((1, DSA_HEADS, r, c), lambda i: (i, 0, 0, 0)),
        out_shape=jax.ShapeDtypeStruct((n, DSA_HEADS, r, c), F32),
        compiler_params=_cparams(("arbitrary",)),
        name="t5tiles",
    )(t5_bias, idx)


def _key_to_f32(m):
    bits = m ^ ((m >> 31) & jnp.int32(0x7FFFFFFF))
    return pltpu.bitcast(bits, F32)


KEY_NEG_INF = int(np.int32(-2139095041))
KEY_POS_INF = 0x7F800000


def _avg_floor(lo, hi):
    return (lo >> 1) + (hi >> 1) + (lo & hi & 1)


DSA_QB = 256
DSA_KV_STEP = 2048


def _dsa_prompt_kernel(qiT_ref, wiT_ref, ki_ref, qT_ref, k_ref, vT_ref, tb_ref, far_ref, g_ref,
                       o_ref, sc_scr, thr_scr, m_scr, l_scr, acc_scr, *, topk, n_sub):
    i = pl.program_id(0)
    j = pl.program_id(1)
    QB = DSA_QB
    NV = QB // SUBLANE
    ntile = i + 1
    shape3 = (NV, SUBLANE, QB)
    key3 = (lax.broadcasted_iota(I32, shape3, 0) * SUBLANE + lax.broadcasted_iota(I32, shape3, 1))
    qry3 = lax.broadcasted_iota(I32, shape3, 2)

    def tile3(t):
        c0 = pl.multiple_of(t * QB, QB)
        return sc_scr[pl.ds(c0, QB), :].reshape(shape3)

    def count(pred3_of_tile):
        def body(t, cnt8):
            return cnt8 + jnp.sum(jnp.where(pred3_of_tile(t), 1.0, 0.0), axis=0)
        cnt8 = lax.fori_loop(0, ntile, body, jnp.zeros((SUBLANE, QB), F32))
        return jnp.sum(cnt8, axis=0, keepdims=True)

    @pl.when(j == 0)
    def _scores_and_threshold():
        wiT = wiT_ref[...]

        def score_tile(t, carry):
            c0 = pl.multiple_of(t * QB, QB)
            kt = ki_ref[pl.ds(c0, QB), :]
            acc = jnp.zeros((QB, QB), F32)
            for h in range(IDX_HEADS):
                s = _dot(kt, qiT_ref[h * IDX_DH:(h + 1) * IDX_DH, :])
                acc = acc + jnp.maximum(s, 0.0) * wiT[h:h + 1, :]
            acc3 = jnp.where((t < i) | (key3 <= qry3), acc.reshape(shape3), -jnp.inf)
            sc_scr[pl.ds(c0, QB), :] = acc3.reshape(QB, QB)
            return carry

        lax.fori_loop(0, ntile, score_tile, 0)

        def bis_cond(c):
            it, n_open, _, _ = c
            return (it < 32) & (n_open > 0)

        def bis_body(c):
            it, _, lo, hi = c
            mid = _avg_floor(lo, hi)
            midf = _key_to_f32(mid)
            cnt = count(lambda t: tile3(t) > midf[None])
            is_open = mid > lo
            ok = cnt <= float(topk - 1)
            exact = cnt == float(topk)
            lo2 = jnp.where(is_open & ~ok, mid, lo)
            hi2 = jnp.where(is_open & ok, mid, jnp.where(is_open & exact, mid + 1, hi))
            n_open = jnp.max(jnp.where(_avg_floor(lo2, hi2) > lo2, 1, 0))
            return it + 1, n_open, lo2, hi2

        lo0 = jnp.full((SUBLANE, QB), KEY_NEG_INF - 1, I32)
        hi0 = jnp.full((SUBLANE, QB), KEY_POS_INF, I32)
        _, _, _, hi = lax.while_loop(bis_cond, bis_body, (jnp.int32(0), jnp.int32(1), lo0, hi0))
        thr8 = _key_to_f32(hi)
        thr_scr[...] = thr8

        n_ge = count(lambda t: tile3(t) >= thr8[None])
        excess = jnp.where((thr8[:1] > -jnp.inf) & (n_ge > float(topk)), 1.0, 0.0)

        @pl.when(jnp.max(excess) > 0.0)
        def _ties():
            need = float(topk) - count(lambda t: tile3(t) > thr8[None])

            def pbis(_, lohi):
                lo, hi = lohi
                mid = (lo + hi) >> 1
                n = count(lambda t: (tile3(t) == thr8[None]) & (key3 + t * QB <= mid[None]))
                ok = n >= need
                return jnp.where(ok, lo, mid), jnp.where(ok, mid, hi)

            n_keys = sc_scr.shape[0]
            nbits = int(math.ceil(math.log2(n_keys))) + 1
            plo = jnp.full((SUBLANE, QB), -1, I32)
            phi = jnp.full((SUBLANE, QB), n_keys - 1, I32)
            _, pcut = lax.fori_loop(0, nbits, pbis, (plo, phi))
            pcut = jnp.where(excess > 0.0, pcut, n_keys)

            def drop(t, carry):
                c0 = pl.multiple_of(t * QB, QB)
                s3 = tile3(t)
                s3 = jnp.where((s3 == thr8[None]) & (key3 + t * QB > pcut[None]), -jnp.inf, s3)
                sc_scr[pl.ds(c0, QB), :] = s3.reshape(QB, QB)
                return carry

            lax.fori_loop(0, ntile, drop, 0)

        m_scr[...] = jnp.full(m_scr.shape, NEG, F32)
        l_scr[...] = jnp.zeros(l_scr.shape, F32)
        acc_scr[...] = jnp.zeros(acc_scr.shape, F32)

    thr8 = thr_scr[...]

    def attend(sub, kind):
        sel = tile3(j * n_sub + sub) >= thr8[None]
        if kind == 2:
            sel = sel & (key3 <= qry3)
        r0 = sub * QB
        for h in range(DSA_HEADS):
            hs = slice(h * DSA_DH, (h + 1) * DSA_DH)
            logit = _dot(k_ref[r0:r0 + QB, hs], qT_ref[hs, :])
            if kind == 0:
                logit = logit + far_ref[h]
            else:
                logit = logit + tb_ref[kind - 1, h]
            l3 = jnp.where(sel, logit.reshape(shape3), NEG)
            m_old = m_scr[h]
            m_new = jnp.maximum(m_old, jnp.max(jnp.max(l3, axis=0), axis=0, keepdims=True))
            alpha = jnp.exp(m_old - m_new)
            p3 = jnp.exp(l3 - m_new[None])
            l_scr[h] = alpha * l_scr[h] + jnp.sum(p3, axis=0)
            pv = _dot(vT_ref[hs, r0:r0 + QB], p3.reshape(QB, QB).astype(BF16))
            acc3 = acc_scr[hs, :].reshape(DSA_DH // SUBLANE, SUBLANE, QB) * alpha[None]
            acc_scr[hs, :] = acc3.reshape(DSA_DH, QB) + pv
            m_scr[h] = m_new

    for sub in range(n_sub):
        t = j * n_sub + sub

        @pl.when(t < i - 1)
        def _far():
            attend(sub, 0)

        @pl.when(t == i - 1)
        def _prev():
            attend(sub, 1)

        @pl.when(t == i)
        def _diag():
            attend(sub, 2)

    @pl.when(j == pl.num_programs(1) - 1)
    def _finish():
        outs = []
        for h in range(DSA_HEADS):
            hs = slice(h * DSA_DH, (h + 1) * DSA_DH)
            l_tot = jnp.sum(l_scr[h], axis=0, keepdims=True)
            outs.append(acc_scr[hs, :] / l_tot)
        oT = jnp.concatenate(outs, axis=0)
        inv = lax.rsqrt(jnp.mean(oT * oT, axis=0, keepdims=True) + EPS)
        gcol = jnp.concatenate([g_ref[...]] * (QB // LANE), axis=1)
        o_ref[...] = (oT * inv * gcol).T.astype(BF16)


def _dsa_prompt(qiT, wiT, ki, qT, k, vT, t5_bias, g_dsa):
    T = k.shape[0]
    QB = DSA_QB
    topk = min(DSA_TOPK_MAX, T // 4)
    kv_step = min(DSA_KV_STEP, T)
    assert T % QB == 0 and T % kv_step == 0 and kv_step % QB == 0
    nq, nkv = T // QB, T // kv_step
    kk = np.arange(QB)[:, None]
    qq = np.arange(QB)[None, :]
    tb = _t5_tiles(t5_bias, np.stack([QB + qq - kk, qq - kk]))
    far = t5_bias[REL_BUCKETS - 1]
    last_kv = lambda i: ((i + 1) * QB - 1) // kv_step
    kv_idx = lambda i, j: jnp.minimum(j, last_kv(i))
    qcol = lambda r: pl.BlockSpec((r, QB), lambda i, j: (0, i))
    g_rep = jnp.broadcast_to(g_dsa.reshape(DSA_W, 1), (DSA_W, LANE))
    return pl.pallas_call(
        functools.partial(_dsa_prompt_kernel, topk=topk, n_sub=kv_step // QB),
        grid=(nq, nkv),
        in_specs=[qcol(IDX_W), qcol(IDX_HEADS), pl.BlockSpec((T, IDX_DH), lambda i, j: (0, 0)),
                  qcol(DSA_W),
                  pl.BlockSpec((kv_step, DSA_W), lambda i, j: (kv_idx(i, j), 0)),
                  pl.BlockSpec((DSA_W, kv_step), lambda i, j: (0, kv_idx(i, j))),
                  pl.BlockSpec((2, DSA_HEADS, QB, QB), lambda i, j: (0, 0, 0, 0)),
                  pl.BlockSpec(memory_space=pltpu.SMEM),
                  pl.BlockSpec((DSA_W, LANE), lambda i, j: (0, 0))],
        out_specs=pl.BlockSpec((QB, DSA_W), lambda i, j: (i, 0)),
        out_shape=jax.ShapeDtypeStruct((T, DSA_W), BF16),
        scratch_shapes=[pltpu.VMEM((T, QB), F32), pltpu.VMEM((SUBLANE, QB), F32),
                        pltpu.VMEM((DSA_HEADS, SUBLANE, QB), F32), pltpu.VMEM((DSA_HEADS, SUBLANE, QB), F32),
                        pltpu.VMEM((DSA_W, QB), F32)],
        compiler_params=_cparams(("arbitrary", "arbitrary")),
        name="dsa_p",
    )(qiT, wiT, ki, qT, k, vT, tb, far, g_rep)


DSA_S_ROWS = 8
DSA_S_GROUP = 8


def _dsa_sample_kernel(pt_ref, qi_ref, wi_ref, kin_ref, q_ref, kn_ref, vn_ref, tb_ref, far_ref, g_ref,
                       ckidx_hbm, ck_hbm, cv_hbm, o_ref,
                       kidx_buf, sc_scr, kbuf, vbuf, m_scr, l_scr, acc_scr, sem_i, sem_k, sem_v,
                       *, topk, n_new, layer):
    b = pl.program_id(0)
    n_pages = pt_ref.shape[1]
    PS = kidx_buf.shape[0] // n_pages
    past = n_pages * PS
    R = DSA_S_ROWS
    G = DSA_S_GROUP
    GK = G * PS
    n_groups = n_pages // G
    L = past + LANE

    def kidx_copy(p):
        return pltpu.make_async_copy(ckidx_hbm.at[layer, pt_ref[b, p]], kidx_buf.at[pl.ds(p * PS, PS)], sem_i)

    def kv_copies(g, slot, p):
        page = pt_ref[b, g * G + p]
        return (pltpu.make_async_copy(ck_hbm.at[layer, page], kbuf.at[slot, pl.ds(p * PS, PS)], sem_k.at[slot]),
                pltpu.make_async_copy(cv_hbm.at[layer, page], vbuf.at[slot, pl.ds(p * PS, PS)], sem_v.at[slot]))

    def start_group(g, slot):
        def body(p, c):
            ck, cv = kv_copies(g, slot, p)
            ck.start()
            cv.start()
            return c
        lax.fori_loop(0, G, body, 0)

    def wait_group(g, slot):
        def body(p, c):
            ck, cv = kv_copies(g, slot, p)
            ck.wait()
            cv.wait()
            return c
        lax.fori_loop(0, G, body, 0)

    def istart(p, c):
        kidx_copy(p).start()
        return c

    def iwait(p, c):
        kidx_copy(p).wait()
        return c

    lax.fori_loop(0, n_pages, istart, 0)
    start_group(0, 0)
    lax.fori_loop(0, n_pages, iwait, 0)

    qi = qi_ref[0]
    wi = wi_ref[0]

    def head_sum(s):
        r = jnp.maximum(s, 0.0) * wi
        tot = r[0:R]
        for h in range(1, IDX_HEADS):
            tot = tot + r[h * R:(h + 1) * R]
        return tot

    def score_chunk(cidx, c):
        c0 = pl.multiple_of(cidx * GK, GK)
        kc = kidx_buf[pl.ds(c0, GK), :].astype(BF16)
        sc_scr[:, pl.ds(c0, GK)] = head_sum(_dot_nt(qi, kc))
        return c

    lax.fori_loop(0, n_groups, score_chunk, 0)
    trow = lax.broadcasted_iota(I32, (R, LANE), 0)
    tcol = lax.broadcasted_iota(I32, (R, LANE), 1)
    new_ok = (tcol <= trow) & (tcol < n_new)
    sc_scr[:, past:L] = jnp.where(new_ok, head_sum(_dot_nt(qi, kin_ref[0])), -jnp.inf)

    def count(pred):
        return jnp.sum(jnp.where(pred, 1.0, 0.0), axis=1, keepdims=True)

    def bis(_, lohi):
        lo, hi = lohi
        mid = _avg_floor(lo, hi)
        ok = (count(sc_scr[...] > _key_to_f32(mid)[:, :1]) <= float(topk - 1)) & (mid > lo)
        return jnp.where(ok, lo, mid), jnp.where(ok, mid, hi)

    lo0 = jnp.full((R, LANE), KEY_NEG_INF - 1, I32)
    hi0 = jnp.full((R, LANE), KEY_POS_INF, I32)
    _, hi = lax.fori_loop(0, 32, bis, (lo0, hi0))
    thr = _key_to_f32(hi)[:, :1]
    excess = jnp.where((thr > -jnp.inf) & (count(sc_scr[...] >= thr) > float(topk)), 1.0, 0.0)

    @pl.when(jnp.max(excess) > 0.0)
    def _ties():
        need = float(topk) - count(sc_scr[...] > thr)
        pos = lax.broadcasted_iota(I32, (R, L), 1)

        def pbis(_, lohi):
            lo, hi = lohi
            mid = (lo + hi) >> 1
            ok = count((sc_scr[...] == thr) & (pos <= mid)) >= need
            return jnp.where(ok, lo, mid), jnp.where(ok, mid, hi)

        nbits = int(math.ceil(math.log2(L))) + 1
        _, pcut = lax.fori_loop(0, nbits, pbis, (jnp.full((R, 1), -1, I32), jnp.full((R, 1), L - 1, I32)))
        pcut = jnp.where(excess > 0.0, pcut, L)
        s = sc_scr[...]
        sc_scr[...] = jnp.where((s == thr) & (pos > pcut), -jnp.inf, s)

    m_scr[...] = jnp.full(m_scr.shape, NEG, F32)
    l_scr[...] = jnp.zeros(l_scr.shape, F32)
    acc_scr[...] = jnp.zeros(acc_scr.shape, F32)
    q = q_ref[0]

    def softmax_step(h, logit, sel, vh):
        hs = slice(h * DSA_DH, (h + 1) * DSA_DH)
        logit = jnp.where(sel, logit, NEG)
        m_old = m_scr[h]
        m_new = jnp.maximum(m_old, jnp.max(logit, axis=1, keepdims=True))
        alpha = jnp.exp(m_old - m_new)
        p = jnp.exp(logit - m_new[:, :1])
        l_scr[h] = alpha * l_scr[h] + jnp.sum(p, axis=1, keepdims=True)
        acc_scr[:, hs] = alpha * acc_scr[:, hs] + _dot(p.astype(BF16), vh)
        m_scr[h] = m_new

    def attend_group(g, slot, last):
        c0 = pl.multiple_of(g * GK, GK)
        sel = sc_scr[:, pl.ds(c0, GK)] >= thr
        for h in range(DSA_HEADS):
            hs = slice(h * DSA_DH, (h + 1) * DSA_DH)
            logit = _dot_nt(q[:, hs], kbuf[slot, :, h, :].astype(BF16))
            if last:
                bias = jnp.concatenate([jnp.full((R, GK - PS), far_ref[h], F32), tb_ref[0, h]], axis=1)
                logit = logit + bias
            else:
                logit = logit + far_ref[h]
            softmax_step(h, logit, sel, vbuf[slot, :, h, :].astype(BF16))

    def group_body(g, c):
        slot = g % 2
        wait_group(g, slot)

        @pl.when(g + 1 < n_groups)
        def _():
            start_group(g + 1, 1 - slot)

        @pl.when(g < n_groups - 1)
        def _():
            attend_group(g, slot, False)

        @pl.when(g == n_groups - 1)
        def _():
            attend_group(g, slot, True)
        return c

    lax.fori_loop(0, n_groups, group_body, 0)

    sel_new = (sc_scr[:, past:L] >= thr) & new_ok
    kn = kn_ref[0]
    vn = vn_ref[0]
    for h in range(DSA_HEADS):
        hs = slice(h * DSA_DH, (h + 1) * DSA_DH)
        softmax_step(h, _dot_nt(q[:, hs], kn[:, hs]) + tb_ref[1, h], sel_new, vn[:, hs])

    outs = []
    for h in range(DSA_HEADS):
        hs = slice(h * DSA_DH, (h + 1) * DSA_DH)
        outs.append(acc_scr[:, hs] / l_scr[h])
    o_ref[0] = _rms(jnp.concatenate(outs, axis=1), g_ref[...]).astype(BF16)


def _dsa_sample(page_table, qi, wi, ki_new, q, k_new, v_new, cache_kidx, cache_k, cache_v, t5_bias, g_dsa,
                n_new, layer):
    Bd, n_pages = page_table.shape
    PS = cache_kidx.shape[2]
    past = n_pages * PS
    R = DSA_S_ROWS
    topk = min(DSA_TOPK_MAX, (past + n_new) // 4)
    assert n_pages % DSA_S_GROUP == 0 and PS == LANE and n_new <= R
    t = np.arange(R)[:, None]
    c = np.arange(LANE)[None, :]
    tb = _t5_tiles(t5_bias, np.stack([PS + t - c, t - c]))
    far = t5_bias[REL_BUCKETS - 1]
    GK = DSA_S_GROUP * PS
    per_b = lambda shp: pl.BlockSpec((1,) + shp, lambda b, pt: (b,) + (0,) * len(shp))
    grid_spec = pltpu.PrefetchScalarGridSpec(
        num_scalar_prefetch=1,
        grid=(Bd,),
        in_specs=[per_b((IDX_HEADS * R, IDX_DH)), per_b((IDX_HEADS * R, 1)), per_b((LANE, IDX_DH)),
                  per_b((R, DSA_W)), per_b((LANE, DSA_W)), per_b((LANE, DSA_W)),
                  pl.BlockSpec((2, DSA_HEADS, R, LANE), lambda b, pt: (0, 0, 0, 0)),
                  pl.BlockSpec(memory_space=pltpu.SMEM),
                  pl.BlockSpec((1, DSA_W), lambda b, pt: (0, 0)),
                  pl.BlockSpec(memory_space=pl.ANY), pl.BlockSpec(memory_space=pl.ANY),
                  pl.BlockSpec(memory_space=pl.ANY)],
        out_specs=per_b((R, DSA_W)),
        scratch_shapes=[pltpu.VMEM((past, IDX_DH), F32), pltpu.VMEM((R, past + LANE), F32),
                        pltpu.VMEM((2, GK, DSA_HEADS, DSA_DH), F32), pltpu.VMEM((2, GK, DSA_HEADS, DSA_DH), F32),
                        pltpu.VMEM((DSA_HEADS, R, LANE), F32), pltpu.VMEM((DSA_HEADS, R, LANE), F32),
                        pltpu.VMEM((R, DSA_W), F32),
                        pltpu.SemaphoreType.DMA(()), pltpu.SemaphoreType.DMA((2,)),
                        pltpu.SemaphoreType.DMA((2,))],
    )
    return pl.pallas_call(
        functools.partial(_dsa_sample_kernel, topk=topk, n_new=n_new, layer=layer),
        grid_spec=grid_spec,
        out_shape=jax.ShapeDtypeStruct((Bd, R, DSA_W), BF16),
        compiler_params=_cparams(("arbitrary",)),
        name="dsa_s",
    )(page_table, qi, wi, ki_new, q, k_new, v_new, tb, far, g_dsa.reshape(1, DSA_W),
      cache_kidx, cache_k, cache_v)


def _post_kernel(og_ref, od_ref, x_ref, gt_ref, g2_ref, sc_ref, sh_ref, wo_ref, wpq_ref,
                 x1_ref, h2_ref, pq_ref):
    half = og_ref.shape[1]
    y = _dot(og_ref[...], wo_ref[:half, :]) + _dot(od_ref[...], wo_ref[half:, :])
    x1 = x_ref[...] + gt_ref[...] * y
    x1_ref[...] = x1
    h2 = (_rms(x1, g2_ref[...]) * (1.0 + sc_ref[...]) + sh_ref[...]).astype(BF16)
    h2_ref[...] = h2
    pq_ref[...] = _dot(h2, wpq_ref[...]).astype(BF16)


def _post(o_gla, o_dsa, x, gt1, g2, sc2, sh2, w_out, w_pq, tm):
    r, d = x.shape
    mod_spec = (pl.BlockSpec((1, d), lambda i: (0, 0)) if gt1.shape[0] == 1
                else pl.BlockSpec((tm, d), lambda i: (i, 0)))
    full = lambda shp: pl.BlockSpec(shp, lambda i: (0,) * len(shp))
    row = lambda w: pl.BlockSpec((tm, w), lambda i: (i, 0))
    return pl.pallas_call(
        _post_kernel,
        grid=(r // tm,),
        in_specs=[row(o_gla.shape[1]), row(o_dsa.shape[1]), row(d), mod_spec, full((1, d)), mod_spec, mod_spec,
                  full(w_out.shape), full(w_pq.shape)],
        out_specs=[row(d), row(d), row(w_pq.shape[1])],
        out_shape=[jax.ShapeDtypeStruct((r, d), F32), jax.ShapeDtypeStruct((r, d), BF16),
                   jax.ShapeDtypeStruct((r, w_pq.shape[1]), BF16)],
        compiler_params=_cparams(("arbitrary",)),
        name="post",
    )(o_gla, o_dsa, x, gt1, g2.reshape(1, d), sc2, sh2, w_out, w_pq)


PEER_NCAND = PEER_TOPK + 1


def _top_rows(work, n):
    vals = []
    for r in range(n):
        m = jnp.max(work, axis=0, keepdims=True)
        vals.append(m)
        if r + 1 < n:
            work = jnp.where(work == m, -jnp.inf, work)
    return vals


def _peer_sel_kernel(pq_ref, k1_ref, k2_ref, e1_ref, e2_ref, s2_ref, c_ref):
    tn = pq_ref.shape[0]
    n = PEER_NCAND
    npad = -(-n // SUBLANE) * SUBLANE
    for h in range(PEER_HEADS):
        pqh = pq_ref[:, h * PEER_DKEY:(h + 1) * PEER_DKEY]
        s1 = _dot_nt(k1_ref[h], pqh)
        s2 = _dot_nt(k2_ref[h], pqh)
        v1 = _top_rows(s1, n)
        v2 = _top_rows(s2, n)
        v2a = jnp.concatenate(v2 + [jnp.full((npad - n, tn), -jnp.inf, F32)], axis=0)
        cands = [v1[0] + v2a] + [v1[a] + v2a[:SUBLANE] for a in range(1, n)]
        assert all(n // (a + 1) <= SUBLANE for a in range(1, n))
        c = _top_rows(jnp.concatenate(cands, axis=0), n)
        thr = 0.5 * (c[PEER_TOPK - 1] + c[PEER_TOPK])
        z = jnp.ones_like(c[0])
        for r in range(1, PEER_TOPK):
            z = z + jnp.exp(c[r] - c[0])
        e1_ref[h] = jnp.exp(s1 - v1[0])
        e2_ref[h] = jnp.exp(s2 - v2[0]) / z
        s2_ref[h] = s2
        c_ref[h] = thr - s1


def _peer_sel(pq, k1p, k2p, tn):
    n, w = pq.shape
    blk = pl.BlockSpec((PEER_HEADS, PEER_NKEYS, tn), lambda i: (0, 0, i))
    full = lambda shp: pl.BlockSpec(shp, lambda i: (0,) * len(shp))
    shp = jax.ShapeDtypeStruct((PEER_HEADS, PEER_NKEYS, n), F32)
    return pl.pallas_call(
        _peer_sel_kernel,
        grid=(n // tn,),
        in_specs=[pl.BlockSpec((tn, w), lambda i: (i, 0)), full(k1p.shape), full(k2p.shape)],
        out_specs=[blk] * 4,
        out_shape=[shp] * 4,
        compiler_params=_cparams(("arbitrary",)),
        name="peer_sel",
    )(pq, k1p, k2p)


PEER_TN = 512
PEER_ROWS = 4


def _peer_kernel(h2_ref, e1_ref, e2_ref, s2_ref, c_ref, u_ref, vt_ref, x1_ref, gt_ref, gf_ref,
                 y_ref, acc_scr, *, final):
    j = pl.program_id(1)

    @pl.when(j == 0)
    def _():
        acc_scr[...] = jnp.zeros(acc_scr.shape, F32)

    aT = _dot_nt(u_ref[...], h2_ref[...])
    w_rows = []
    for r in range(PEER_ROWS):
        i = j * PEER_ROWS + r
        gate = jnp.zeros((PEER_NKEYS, aT.shape[1]), F32)
        for h in range(PEER_HEADS):
            e1row = e1_ref[h, pl.ds(i, 1), :]
            crow = c_ref[h, pl.ds(i, 1), :]
            gate = gate + e1row * jnp.where(s2_ref[h] >= crow, e2_ref[h], 0.0)
        a = aT[r * PEER_NKEYS:(r + 1) * PEER_NKEYS]
        gelu = 0.5 * a * (1.0 + lax.erf(a * (2.0 ** -0.5)))
        w_rows.append((gate * gelu).astype(BF16))
    wT = jnp.concatenate(w_rows, axis=0)
    acc_scr[...] += _dot(vt_ref[...], wT)

    @pl.when(j == pl.num_programs(1) - 1)
    def _():
        x2 = x1_ref[...] + gt_ref[...] * acc_scr[...].T
        y_ref[...] = _rms(x2, gf_ref[...]) if final else x2


def _peer(h2, e1, e2, s2, c, u_b, vt_b, x1, gt2, g_final, final):
    n, d = x1.shape
    tn = PEER_TN
    te = PEER_ROWS * PEER_NKEYS
    n_exp = u_b.shape[0]
    mod_spec = (pl.BlockSpec((1, d), lambda i, j: (0, 0)) if gt2.shape[0] == 1
                else pl.BlockSpec((tn, d), lambda i, j: (i, 0)))
    sel = pl.BlockSpec((PEER_HEADS, PEER_NKEYS, tn), lambda i, j: (0, 0, i))
    row = pl.BlockSpec((tn, d), lambda i, j: (i, 0))
    return pl.pallas_call(
        functools.partial(_peer_kernel, final=final),
        grid=(n // tn, n_exp // te),
        in_specs=[row, sel, sel, sel, sel,
                  pl.BlockSpec((te, d), lambda i, j: (j, 0)),
                  pl.BlockSpec((d, te), lambda i, j: (0, j)),
                  row, mod_spec, pl.BlockSpec((1, d), lambda i, j: (0, 0))],
        out_specs=row,
        out_shape=jax.ShapeDtypeStruct((n, d), F32),
        scratch_shapes=[pltpu.VMEM((d, tn), F32)],
        compiler_params=_cparams(("arbitrary", "arbitrary")),
        name="peer",
    )(h2, e1, e2, s2, c, u_b, vt_b, x1, gt2, g_final.reshape(1, d))


def _pad_in_proj(w_in):
    parts, o = [], 0
    for s, ps in zip(SPLIT_SIZES, _PAD_SIZES):
        parts.append(jnp.pad(w_in[:, o:o + s], ((0, 0), (0, ps - s))))
        o += s
    return jnp.concatenate(parts, axis=1).astype(BF16)


def _pad_peer_keys(keys, lo):
    half = keys.shape[-1]
    return jnp.pad(keys, ((0, 0), (0, 0), (lo, PEER_DKEY - half - lo))).astype(BF16)


def _pad_axis(a, axis, size):
    pad = [(0, 0)] * a.ndim
    pad[axis] = (0, size - a.shape[axis])
    return jnp.pad(a, pad)


def _rows_mod(mod, reps):
    parts = jnp.split(mod, 6, axis=-1)
    if mod.shape[0] == 1:
        return parts
    return [jnp.repeat(p, reps, axis=0) for p in parts]


def kernel(x_prompt, x_sample, cache_k, cache_v, cache_kidx, state_gla, page_table, c_prompt, c_sample,
           w_ada, b_ada, g_norm1, g_norm2, w_in, w_gate_a2, b_gate_a2, g_gla_out, g_dsa_out, w_out,
           t5_bias, w_peer_q, peer_keys1, peer_keys2, peer_u, peer_v, g_final):
    depth = w_ada.shape[0]
    B, T, D = x_prompt.shape
    Bd, Ts, _ = x_sample.shape
    xp = x_prompt.reshape(B * T, D)
    xs = x_sample.reshape(Bd * Ts, D)
    R = DSA_S_ROWS
    outs = [[] for _ in range(8)]
    for l in range(depth):
        w_in_p = _pad_in_proj(w_in[l])
        wa2_p = _pad_axis(w_gate_a2[l], 0, LANE).astype(BF16)
        w_out_b = w_out[l].astype(BF16)
        w_pq_b = w_peer_q[l].astype(BF16)
        k1p = _pad_peer_keys(peer_keys1[l], 0)
        k2p = _pad_peer_keys(peer_keys2[l], PEER_DKEY // 2)
        u_b = peer_u[l].astype(BF16)
        vt_b = peer_v[l].T.astype(BF16)
        last = l == depth - 1

        c_all = jnp.concatenate([c_prompt, c_sample], axis=0)
        c_all = _pad_axis(c_all, 0, -(-(B + Bd) // SUBLANE) * SUBLANE)
        mod = _ada(c_all, w_ada[l], b_ada[l])
        mod_p = _rows_mod(mod[:B], T)
        mod_s = _rows_mod(mod[B:B + Bd], Ts)

        sh1, sc1, gt1, sh2, sc2, gt2 = mod_p
        (gq, gk, gv, gg, lg, dq, dk, dv, dkb, dvb, qi, ki, kib, wi) = _proj(
            xp, g_norm1[l], sc1, sh1, w_in_p, wa2_p, b_gate_a2[l], 256)
        s0p = jnp.zeros((B, GLA_HEADS, GLA_DK, GLA_DV), F32)
        r3 = lambda a: a.reshape(B, T, a.shape[-1])
        o_gla, st_p = _gla(r3(gq), r3(gk), r3(gv), r3(gg), r3(lg), g_gla_out[l], s0p, min(64, T))
        seq = lambda a, b: a[b * T:(b + 1) * T]
        o_dsa = jnp.concatenate([
            _dsa_prompt(seq(qi, b).T, seq(wi, b)[:, :IDX_HEADS].T, seq(kib, b),
                        seq(dq, b).T, seq(dkb, b), seq(dvb, b).T, t5_bias, g_dsa_out[l])
            for b in range(B)], axis=0)
        x1, h2, pq = _post(o_gla.reshape(B * T, GLA_W), o_dsa, xp, gt1, g_norm2[l], sc2, sh2, w_out_b, w_pq_b, 256)
        e1, e2, s2, cth = _peer_sel(pq, k1p, k2p, 256)
        xp = _peer(h2, e1, e2, s2, cth, u_b, vt_b, x1, gt2, g_final, last)
        outs[0].append(dk.reshape(B, T, DSA_HEADS, DSA_DH))
        outs[1].append(dv.reshape(B, T, DSA_HEADS, DSA_DH))
        outs[2].append(ki.reshape(B, T, IDX_DH))
        outs[3].append(st_p)

        sh1, sc1, gt1, sh2, sc2, gt2 = mod_s
        (gq, gk, gv, gg, lg, dq, dk, dv, dkb, dvb, qi, ki, kib, wi) = _proj(
            xs, g_norm1[l], sc1, sh1, w_in_p, wa2_p, b_gate_a2[l], 256)
        tokpad = lambda a, n: _pad_axis(a.reshape(Bd, Ts, a.shape[-1]), 1, n)
        o_gla, st_s = _gla(tokpad(gq, R), tokpad(gk, R), tokpad(gv, R), tokpad(gg, R), tokpad(lg, R),
                           g_gla_out[l], state_gla[l], R)
        qi_s = _pad_axis(qi.reshape(Bd, Ts, IDX_HEADS, IDX_DH).transpose(0, 2, 1, 3), 2, R)
        wi_s = _pad_axis(wi[:, :IDX_HEADS].reshape(Bd, Ts, IDX_HEADS).transpose(0, 2, 1), 2, R)
        o_dsa = _dsa_sample(page_table, qi_s.reshape(Bd, IDX_HEADS * R, IDX_DH),
                            wi_s.reshape(Bd, IDX_HEADS * R, 1), tokpad(kib, LANE), tokpad(dq, R),
                            tokpad(dkb, LANE), tokpad(dvb, LANE),
                            cache_kidx, cache_k, cache_v, t5_bias, g_dsa_out[l], Ts, l)
        x1, h2, pq = _post(o_gla[:, :Ts].reshape(Bd * Ts, GLA_W), o_dsa[:, :Ts].reshape(Bd * Ts, DSA_W),
                           xs, gt1, g_norm2[l], sc2, sh2, w_out_b, w_pq_b, 256)
        e1, e2, s2, cth = _peer_sel(pq, k1p, k2p, 256)
        xs = _peer(h2, e1, e2, s2, cth, u_b, vt_b, x1, gt2, g_final, last)
        outs[4].append(dk.reshape(Bd, Ts, DSA_HEADS, DSA_DH))
        outs[5].append(dv.reshape(Bd, Ts, DSA_HEADS, DSA_DH))
        outs[6].append(ki.reshape(Bd, Ts, IDX_DH))
        outs[7].append(st_s)

    return (xp.reshape(B, T, D), xs.reshape(Bd, Ts, D)) + tuple(jnp.stack(o) for o in outs)
```

```python
import functools
import math

import numpy as np
import jax
import jax.numpy as jnp
from jax import lax
from jax.experimental import pallas as pl
from jax.experimental.pallas import tpu as pltpu

F32 = jnp.float32
BF16 = jnp.bfloat16
I32 = jnp.int32

EPS = 1e-6
GLA_HEADS, GLA_DK, GLA_DV = 4, 64, 128
GLA_GATE_RANK = 16
GLA_GATE_NORM = 16.0
GLA_QK = GLA_HEADS * GLA_DK
GLA_W = GLA_HEADS * GLA_DV
DSA_HEADS, DSA_DH = 4, 128
DSA_W = DSA_HEADS * DSA_DH
IDX_HEADS, IDX_DH = 8, 64
IDX_W = IDX_HEADS * IDX_DH
DSA_TOPK_MAX = 256
REL_BUCKETS, REL_MAX_DIST = 32, 128
PEER_HEADS, PEER_NKEYS, PEER_DKEY, PEER_TOPK = 8, 128, 128, 16
SPLIT_SIZES = (GLA_QK, GLA_QK, GLA_W, GLA_W, GLA_GATE_RANK,
               DSA_W, DSA_W, DSA_W, IDX_W, IDX_DH, IDX_HEADS)

LANE = 128
SUBLANE = 8
NEG = -1e30
VMEM_LIMIT = 56 * 1024 * 1024

_PAD_SIZES = tuple(-(-s // LANE) * LANE for s in SPLIT_SIZES)
_PAD_OFFS = tuple(int(v) for v in np.cumsum((0,) + _PAD_SIZES))
(O_GQ, O_GK, O_GV, O_GG, O_GA, O_DQ, O_DK, O_DV, O_QI, O_KI, O_WI, D_IN_PAD) = _PAD_OFFS


def _cparams(sem):
    return pltpu.CompilerParams(dimension_semantics=sem, vmem_limit_bytes=VMEM_LIMIT)


def _dot(a, b):
    return jnp.dot(a, b, preferred_element_type=F32)


def _dot_nt(a, b):
    return lax.dot_general(a, b, (((1,), (1,)), ((), ())), preferred_element_type=F32)


def _rms(x, g):
    return x * lax.rsqrt(jnp.mean(x * x, axis=-1, keepdims=True) + EPS) * g


def _ada_kernel(c_ref, w_ref, b_ref, o_ref):
    c = c_ref[...]
    s = c * jax.nn.sigmoid(c)
    o_ref[...] = _dot(s.astype(BF16), w_ref[...].astype(BF16)) + b_ref[...]


def _ada(c, w, b):
    m, d = c.shape
    n = w.shape[1]
    tn = 1024
    return pl.pallas_call(
        _ada_kernel,
        grid=(n // tn,),
        in_specs=[pl.BlockSpec((m, d), lambda j: (0, 0)),
                  pl.BlockSpec((d, tn), lambda j: (0, j)),
                  pl.BlockSpec((1, tn), lambda j: (0, j))],
        out_specs=pl.BlockSpec((m, tn), lambda j: (0, j)),
        out_shape=jax.ShapeDtypeStruct((m, n), F32),
        compiler_params=_cparams(("arbitrary",)),
        name="ada",
    )(c, w, b.reshape(1, n))


def _proj_kernel(x_ref, g_ref, sc_ref, sh_ref, w_ref, wa2_ref, ba2_ref,
                 q_ref, k_ref, v_ref, gg_ref, lg_ref, dq_ref, dk_ref, dv_ref,
                 dkb_ref, dvb_ref, qi_ref, ki_ref, kib_ref, wi_ref):
    x = x_ref[...]
    h = _rms(x, g_ref[...]) * (1.0 + sc_ref[...]) + sh_ref[...]
    p = _dot(h.astype(BF16), w_ref[...])
    q_ref[...] = p[:, O_GQ:O_GK] * (GLA_DK ** -0.5)
    k_ref[...] = p[:, O_GK:O_GV]
    v_ref[...] = p[:, O_GV:O_GG]
    gg_ref[...] = p[:, O_GG:O_GA]
    z = _dot(p[:, O_GA:O_DQ].astype(BF16), wa2_ref[...]) + ba2_ref[...]
    lg_ref[...] = (jnp.minimum(z, 0.0) - jnp.log(1.0 + jnp.exp(-jnp.abs(z)))) * (1.0 / GLA_GATE_NORM)
    dq_ref[...] = (p[:, O_DQ:O_DK] * (DSA_DH ** -0.5)).astype(BF16)
    dk = p[:, O_DK:O_DV]
    dv = p[:, O_DV:O_QI]
    dk_ref[...] = dk
    dv_ref[...] = dv
    dkb_ref[...] = dk.astype(BF16)
    dvb_ref[...] = dv.astype(BF16)
    qi_ref[...] = (p[:, O_QI:O_KI] * (IDX_DH ** -0.5)).astype(BF16)
    ki = p[:, O_KI:O_KI + IDX_DH]
    ki_ref[...] = ki
    kib_ref[...] = ki.astype(BF16)
    wi_ref[...] = p[:, O_WI:D_IN_PAD] * (IDX_HEADS ** -0.5)


def _proj(x, g1, sc, sh, w_in_p, wa2_p, ba2, tm):
    r, d = x.shape
    rows_mod = sc.shape[0]
    mod_spec = (pl.BlockSpec((1, d), lambda i: (0, 0)) if rows_mod == 1
                else pl.BlockSpec((tm, d), lambda i: (i, 0)))
    full = lambda shp: pl.BlockSpec(shp, lambda i: (0,) * len(shp))
    widths = [(GLA_QK, F32), (GLA_QK, F32), (GLA_W, F32), (GLA_W, F32), (GLA_QK, F32),
              (DSA_W, BF16), (DSA_W, F32), (DSA_W, F32), (DSA_W, BF16), (DSA_W, BF16),
              (IDX_W, BF16), (IDX_DH, F32), (IDX_DH, BF16), (LANE, F32)]
    return pl.pallas_call(
        _proj_kernel,
        grid=(r // tm,),
        in_specs=[pl.BlockSpec((tm, d), lambda i: (i, 0)), full((1, d)), mod_spec, mod_spec,
                  full(w_in_p.shape), full(wa2_p.shape), full((1, GLA_QK))],
        out_specs=[pl.BlockSpec((tm, w), lambda i: (i, 0)) for w, _ in widths],
        out_shape=[jax.ShapeDtypeStruct((r, w), dt) for w, dt in widths],
        compiler_params=_cparams(("arbitrary",)),
        name="proj",
    )(x, g1.reshape(1, d), sc, sh, w_in_p, wa2_p, ba2.reshape(1, GLA_QK))


GLA_SUB = 16
GLA_CHUNK = 128


def _gla_kernel(q_ref, k_ref, v_ref, gg_ref, lg_ref, go_ref, s0_ref, o_ref, s_ref, st_scr):
    c = pl.program_id(1)

    @pl.when(c == 0)
    def _():
        st_scr[...] = s0_ref[0]

    q = q_ref[0]
    k = k_ref[0]
    v = v_ref[0]
    lg = lg_ref[0]
    C = q.shape[0]
    rows = lax.broadcasted_iota(I32, (C, C), 0)
    cols = lax.broadcasted_iota(I32, (C, C), 1)
    tri = (rows >= cols).astype(F32)
    b = jnp.dot(tri, lg, preferred_element_type=F32, precision=lax.Precision.HIGHEST)
    b_last = b[C - 1:C, :]
    qc = q * jnp.exp(b)
    kc = k * jnp.exp(b_last - b)
    kcT = kc.T
    dec_col = jnp.broadcast_to(jnp.exp(jnp.sum(lg.T, axis=1, keepdims=True)), (GLA_QK, GLA_DV))
    go = go_ref[...]
    outs = []
    for h in range(GLA_HEADS):
        sl = slice(h * GLA_DK, (h + 1) * GLA_DK)
        vh = v[:, h * GLA_DV:(h + 1) * GLA_DV].astype(BF16)
        S = st_scr[h]
        o = _dot(qc[:, sl].astype(BF16), S.astype(BF16))
        o_rows = []
        sub = min(GLA_SUB, C)
        for blk in range(C // sub):
            r0, r1 = blk * sub, (blk + 1) * sub
            bref = b[r0 - 1:r0, sl] if blk > 0 else jnp.zeros((1, GLA_DK), F32)
            qt = q[r0:r1, sl] * jnp.exp(b[r0:r1, sl] - bref)
            kt = k[:r1, sl] * jnp.exp(bref - b[:r1, sl])
            a = _dot_nt(qt.astype(BF16), kt.astype(BF16))
            causal = (lax.broadcasted_iota(I32, (sub, r1), 1)
                      <= lax.broadcasted_iota(I32, (sub, r1), 0) + r0)
            a = jnp.where(causal, a, 0.0)
            o_rows.append(_dot(a.astype(BF16), vh[:r1]))
        o = o + (jnp.concatenate(o_rows, axis=0) if len(o_rows) > 1 else o_rows[0])
        st_scr[h] = dec_col[sl] * S + _dot(kcT[sl].astype(BF16), vh)
        outs.append(_rms(o, go))
    o_all = jnp.concatenate(outs, axis=1)
    gg = gg_ref[0]
    o_ref[0] = (o_all * (gg * jax.nn.sigmoid(gg))).astype(BF16)

    @pl.when(c == pl.num_programs(1) - 1)
    def _():
        s_ref[0] = st_scr[...]


def _gla(q, k, v, gg, lg, g_out, s0, chunk):
    B, T, _ = q.shape
    n = T // chunk
    tok = lambda w: pl.BlockSpec((1, chunk, w), lambda b, c: (b, c, 0))
    st = pl.BlockSpec((1, GLA_HEADS, GLA_DK, GLA_DV), lambda b, c: (b, 0, 0, 0))
    return pl.pallas_call(
        _gla_kernel,
        grid=(B, n),
        in_specs=[tok(GLA_QK), tok(GLA_QK), tok(GLA_W), tok(GLA_W), tok(GLA_QK),
                  pl.BlockSpec((1, GLA_DV), lambda b, c: (0, 0)), st],
        out_specs=[tok(GLA_W), st],
        out_shape=[jax.ShapeDtypeStruct((B, T, GLA_W), BF16),
                   jax.ShapeDtypeStruct((B, GLA_HEADS, GLA_DK, GLA_DV), F32)],
        scratch_shapes=[pltpu.VMEM((GLA_HEADS, GLA_DK, GLA_DV), F32)],
        compiler_params=_cparams(("arbitrary", "arbitrary")),
        name="gla",
    )(q, k, v, gg, lg, g_out.reshape(1, GLA_DV), s0)


def _t5_bucket_np(d):
    max_exact = REL_BUCKETS // 2
    d = np.maximum(d, 0)
    large = max_exact + (np.log(np.maximum(d, 1).astype(np.float32) / np.float32(max_exact))
                         / np.float32(math.log(REL_MAX_DIST / max_exact))
                         * np.float32(REL_BUCKETS - max_exact)).astype(np.int32)
    large = np.minimum(large, REL_BUCKETS - 1)
    return np.where(d < max_exact, d, large).astype(np.int32)


T5_FAR = 113
assert int(_t5_bucket_np(np.arange(T5_FAR, 1 << 16)).min()) == REL_BUCKETS - 1


def _t5_kernel(bias_ref, idx_ref, o_ref):
    idx = idx_ref[0]
    for h in range(DSA_HEADS):
        acc = jnp.zeros(idx.shape, F32)
        for bkt in range(REL_BUCKETS):
            acc = jnp.where(idx == bkt, bias_ref[bkt, h], acc)
        o_ref[0, h] = acc


def _t5_tiles(t5_bias, dist_tiles):
    idx = jnp.asarray(_t5_bucket_np(dist_tiles))
    n, r, c = idx.shape
    return pl.pallas_call(
        _t5_kernel,
        grid=(n,),
        in_specs=[pl.BlockSpec(memory_space=pltpu.SMEM),
                  pl.BlockSpec((1, r, c), lambda i: (i, 0, 0))],
        out_specs=pl.BlockSpec((1, DSA_HEADS, r, c), lambda i: (i, 0, 0, 0)),
        out_shape=jax.ShapeDtypeStruct((n, DSA_HEADS, r, c), F32),
        compiler_params=_cparams(("arbitrary",)),
        name="t5tiles",
    )(t5_bias, idx)


def _key_to_f32(m):
    bits = m ^ ((m >> 31) & jnp.int32(0x7FFFFFFF))
    return pltpu.bitcast(bits, F32)


KEY_NEG_INF = int(np.int32(-2139095041))
KEY_POS_INF = 0x7F800000


def _avg_floor(lo, hi):
    return (lo >> 1) + (hi >> 1) + (lo & hi & 1)


DSA_QB = 256
DSA_KV_STEP = 2048


def _dsa_prompt_kernel(qiT_ref, wiT_ref, ki_ref, qT_ref, k_ref, vT_ref, tb_ref, far_ref, g_ref,
                       o_ref, sc_scr, thr_scr, m_scr, l_scr, acc_scr, *, topk, n_sub):
    i = pl.program_id(0)
    j = pl.program_id(1)
    QB = DSA_QB
    NV = QB // SUBLANE
    ntile = i + 1
    shape3 = (NV, SUBLANE, QB)
    key3 = (lax.broadcasted_iota(I32, shape3, 0) * SUBLANE + lax.broadcasted_iota(I32, shape3, 1))
    qry3 = lax.broadcasted_iota(I32, shape3, 2)

    def tile3(t):
        c0 = pl.multiple_of(t * QB, QB)
        return sc_scr[pl.ds(c0, QB), :].reshape(shape3)

    def count(pred3_of_tile):
        def body(t, cnt8):
            return cnt8 + jnp.sum(jnp.where(pred3_of_tile(t), 1.0, 0.0), axis=0)
        cnt8 = lax.fori_loop(0, ntile, body, jnp.zeros((SUBLANE, QB), F32))
        return jnp.sum(cnt8, axis=0, keepdims=True)

    @pl.when(j == 0)
    def _scores_and_threshold():
        wiT = wiT_ref[...]

        def score_tile(t, carry):
            c0 = pl.multiple_of(t * QB, QB)
            kt = ki_ref[pl.ds(c0, QB), :]
            acc = jnp.zeros((QB, QB), F32)
            for h in range(IDX_HEADS):
                s = _dot(kt, qiT_ref[h * IDX_DH:(h + 1) * IDX_DH, :])
                acc = acc + jnp.maximum(s, 0.0) * wiT[h:h + 1, :]
            acc3 = jnp.where((t < i) | (key3 <= qry3), acc.reshape(shape3), -jnp.inf)
            sc_scr[pl.ds(c0, QB), :] = acc3.reshape(QB, QB)
            return carry

        lax.fori_loop(0, ntile, score_tile, 0)

        def bis_cond(c):
            it, n_open, _, _ = c
            return (it < 32) & (n_open > 0)

        def bis_body(c):
            it, _, lo, hi = c
            mid = _avg_floor(lo, hi)
            midf = _key_to_f32(mid)
            cnt = count(lambda t: tile3(t) > midf[None])
            is_open = mid > lo
            ok = cnt <= float(topk - 1)
            exact = cnt == float(topk)
            lo2 = jnp.where(is_open & ~ok, mid, lo)
            hi2 = jnp.where(is_open & ok, mid, jnp.where(is_open & exact, mid + 1, hi))
            n_open = jnp.max(jnp.where(_avg_floor(lo2, hi2) > lo2, 1, 0))
            return it + 1, n_open, lo2, hi2

        lo0 = jnp.full((SUBLANE, QB), KEY_NEG_INF - 1, I32)
        hi0 = jnp.full((SUBLANE, QB), KEY_POS_INF, I32)
        _, _, _, hi = lax.while_loop(bis_cond, bis_body, (jnp.int32(0), jnp.int32(1), lo0, hi0))
        thr8 = _key_to_f32(hi)
        thr_scr[...] = thr8

        n_ge = count(lambda t: tile3(t) >= thr8[None])
        excess = jnp.where((thr8[:1] > -jnp.inf) & (n_ge > float(topk)), 1.0, 0.0)

        @pl.when(jnp.max(excess) > 0.0)
        def _ties():
            need = float(topk) - count(lambda t: tile3(t) > thr8[None])

            def pbis(_, lohi):
                lo, hi = lohi
                mid = (lo + hi) >> 1
                n = count(lambda t: (tile3(t) == thr8[None]) & (key3 + t * QB <= mid[None]))
                ok = n >= need
                return jnp.where(ok, lo, mid), jnp.where(ok, mid, hi)

            n_keys = sc_scr.shape[0]
            nbits = int(math.ceil(math.log2(n_keys))) + 1
            plo = jnp.full((SUBLANE, QB), -1, I32)
            phi = jnp.full((SUBLANE, QB), n_keys - 1, I32)
            _, pcut = lax.fori_loop(0, nbits, pbis, (plo, phi))
            pcut = jnp.where(excess > 0.0, pcut, n_keys)

            def drop(t, carry):
                c0 = pl.multiple_of(t * QB, QB)
                s3 = tile3(t)
                s3 = jnp.where((s3 == thr8[None]) & (key3 + t * QB > pcut[None]), -jnp.inf, s3)
                sc_scr[pl.ds(c0, QB), :] = s3.reshape(QB, QB)
                return carry

            lax.fori_loop(0, ntile, drop, 0)

        m_scr[...] = jnp.full(m_scr.shape, NEG, F32)
        l_scr[...] = jnp.zeros(l_scr.shape, F32)
        acc_scr[...] = jnp.zeros(acc_scr.shape, F32)

    thr8 = thr_scr[...]

    def attend(sub, kind):
        sel = tile3(j * n_sub + sub) >= thr8[None]
        if kind == 2:
            sel = sel & (key3 <= qry3)
        r0 = sub * QB
        for h in range(DSA_HEADS):
            hs = slice(h * DSA_DH, (h + 1) * DSA_DH)
            logit = _dot(k_ref[r0:r0 + QB, hs], qT_ref[hs, :])
            if kind == 0:
                logit = logit + far_ref[h]
            else:
                logit = logit + tb_ref[kind - 1, h]
            l3 = jnp.where(sel, logit.reshape(shape3), NEG)
            m_old = m_scr[h]
            m_new = jnp.maximum(m_old, jnp.max(jnp.max(l3, axis=0), axis=0, keepdims=True))
            alpha = jnp.exp(m_old - m_new)
            p3 = jnp.exp(l3 - m_new[None])
            l_scr[h] = alpha * l_scr[h] + jnp.sum(p3, axis=0)
            pv = _dot(vT_ref[hs, r0:r0 + QB], p3.reshape(QB, QB).astype(BF16))
            acc3 = acc_scr[hs, :].reshape(DSA_DH // SUBLANE, SUBLANE, QB) * alpha[None]
            acc_scr[hs, :] = acc3.reshape(DSA_DH, QB) + pv
            m_scr[h] = m_new

    for sub in range(n_sub):
        t = j * n_sub + sub

        @pl.when(t < i - 1)
        def _far():
            attend(sub, 0)

        @pl.when(t == i - 1)
        def _prev():
            attend(sub, 1)

        @pl.when(t == i)
        def _diag():
            attend(sub, 2)

    @pl.when(j == pl.num_programs(1) - 1)
    def _finish():
        outs = []
        for h in range(DSA_HEADS):
            hs = slice(h * DSA_DH, (h + 1) * DSA_DH)
            l_tot = jnp.sum(l_scr[h], axis=0, keepdims=True)
            outs.append(acc_scr[hs, :] / l_tot)
        oT = jnp.concatenate(outs, axis=0)
        inv = lax.rsqrt(jnp.mean(oT * oT, axis=0, keepdims=True) + EPS)
        gcol = jnp.concatenate([g_ref[...]] * (QB // LANE), axis=1)
        o_ref[...] = (oT * inv * gcol).T.astype(BF16)


def _dsa_prompt(qiT, wiT, ki, qT, k, vT, t5_bias, g_dsa):
    T = k.shape[0]
    QB = DSA_QB
    topk = min(DSA_TOPK_MAX, T // 4)
    kv_step = min(DSA_KV_STEP, T)
    assert T % QB == 0 and T % kv_step == 0 and kv_step % QB == 0
    nq, nkv = T // QB, T // kv_step
    kk = np.arange(QB)[:, None]
    qq = np.arange(QB)[None, :]
    tb = _t5_tiles(t5_bias, np.stack([QB + qq - kk, qq - kk]))
    far = t5_bias[REL_BUCKETS - 1]
    last_kv = lambda i: ((i + 1) * QB - 1) // kv_step
    kv_idx = lambda i, j: jnp.minimum(j, last_kv(i))
    qcol = lambda r: pl.BlockSpec((r, QB), lambda i, j: (0, i))
    g_rep = jnp.broadcast_to(g_dsa.reshape(DSA_W, 1), (DSA_W, LANE))
    return pl.pallas_call(
        functools.partial(_dsa_prompt_kernel, topk=topk, n_sub=kv_step // QB),
        grid=(nq, nkv),
        in_specs=[qcol(IDX_W), qcol(IDX_HEADS), pl.BlockSpec((T, IDX_DH), lambda i, j: (0, 0)),
                  qcol(DSA_W),
                  pl.BlockSpec((kv_step, DSA_W), lambda i, j: (kv_idx(i, j), 0)),
                  pl.BlockSpec((DSA_W, kv_step), lambda i, j: (0, kv_idx(i, j))),
                  pl.BlockSpec((2, DSA_HEADS, QB, QB), lambda i, j: (0, 0, 0, 0)),
                  pl.BlockSpec(memory_space=pltpu.SMEM),
                  pl.BlockSpec((DSA_W, LANE), lambda i, j: (0, 0))],
        out_specs=pl.BlockSpec((QB, DSA_W), lambda i, j: (i, 0)),
        out_shape=jax.ShapeDtypeStruct((T, DSA_W), BF16),
        scratch_shapes=[pltpu.VMEM((T, QB), F32), pltpu.VMEM((SUBLANE, QB), F32),
                        pltpu.VMEM((DSA_HEADS, SUBLANE, QB), F32), pltpu.VMEM((DSA_HEADS, SUBLANE, QB), F32),
                        pltpu.VMEM((DSA_W, QB), F32)],
        compiler_params=_cparams(("arbitrary", "arbitrary")),
        name="dsa_p",
    )(qiT, wiT, ki, qT, k, vT, tb, far, g_rep)


DSA_S_ROWS = 8
DSA_S_GROUP = 8


def _dsa_sample_kernel(pt_ref, qi_ref, wi_ref, kin_ref, q_ref, kn_ref, vn_ref, tb_ref, far_ref, g_ref,
                       ckidx_hbm, ck_hbm, cv_hbm, o_ref,
                       kidx_buf, sc_scr, kbuf, vbuf, m_scr, l_scr, acc_scr, sem_i, sem_k, sem_v,
                       *, topk, n_new, layer):
    b = pl.program_id(0)
    n_pages = pt_ref.shape[1]
    PS = kidx_buf.shape[1] // n_pages
    past = n_pages * PS
    R = DSA_S_ROWS
    G = DSA_S_GROUP
    GK = G * PS
    n_groups = n_pages // G
    L = past + LANE
    PR = PS * DSA_HEADS

    def kidx_copy(p):
        return pltpu.make_async_copy(ckidx_hbm.at[layer, pt_ref[b, p]], kidx_buf.at[:, pl.ds(p * PS, PS)], sem_i)

    def kv_copies(g, slot, p):
        page = pt_ref[b, g * G + p]
        return (pltpu.make_async_copy(ck_hbm.at[layer, page], kbuf.at[slot, pl.ds(p * PR, PR)], sem_k.at[slot]),
                pltpu.make_async_copy(cv_hbm.at[layer, page], vbuf.at[slot, pl.ds(p * PR, PR)], sem_v.at[slot]))

    def start_group(g, slot):
        def body(p, c):
            ck, cv = kv_copies(g, slot, p)
            ck.start()
            cv.start()
            return c
        lax.fori_loop(0, G, body, 0)

    def wait_group(g, slot):
        def body(p, c):
            ck, cv = kv_copies(g, slot, p)
            ck.wait()
            cv.wait()
            return c
        lax.fori_loop(0, G, body, 0)

    def istart(p, c):
        kidx_copy(p).start()
        return c

    def iwait(p, c):
        kidx_copy(p).wait()
        return c

    lax.fori_loop(0, n_pages, istart, 0)
    start_group(0, 0)
    lax.fori_loop(0, n_pages, iwait, 0)

    qi = qi_ref[0]
    wi = wi_ref[0]

    def head_sum(s):
        r = jnp.maximum(s, 0.0) * wi
        tot = r[0:R]
        for h in range(1, IDX_HEADS):
            tot = tot + r[h * R:(h + 1) * R]
        return tot

    def score_chunk(cidx, c):
        c0 = pl.multiple_of(cidx * GK, GK)
        kc = kidx_buf[:, pl.ds(c0, GK)].astype(BF16)
        sc_scr[:, pl.ds(c0, GK)] = head_sum(_dot(qi, kc))
        return c

    lax.fori_loop(0, n_groups, score_chunk, 0)
    trow = lax.broadcasted_iota(I32, (R, LANE), 0)
    tcol = lax.broadcasted_iota(I32, (R, LANE), 1)
    new_ok = (tcol <= trow) & (tcol < n_new)
    sc_scr[:, past:L] = jnp.where(new_ok, head_sum(_dot_nt(qi, kin_ref[0])), -jnp.inf)

    def count(pred):
        return jnp.sum(jnp.where(pred, 1.0, 0.0), axis=1, keepdims=True)

    def bis_cond(c):
        it, n_open, _, _ = c
        return (it < 32) & (n_open > 0)

    def bis_body(c):
        it, _, lo, hi = c
        mid = _avg_floor(lo, hi)
        cnt = count(sc_scr[...] > _key_to_f32(mid)[:, :1])
        is_open = mid > lo
        ok = cnt <= float(topk - 1)
        exact = cnt == float(topk)
        lo2 = jnp.where(is_open & ~ok, mid, lo)
        hi2 = jnp.where(is_open & ok, mid, jnp.where(is_open & exact, mid + 1, hi))
        n_open = jnp.max(jnp.where(_avg_floor(lo2, hi2) > lo2, 1, 0))
        return it + 1, n_open, lo2, hi2

    lo0 = jnp.full((R, LANE), KEY_NEG_INF - 1, I32)
    hi0 = jnp.full((R, LANE), KEY_POS_INF, I32)
    _, _, _, hi = lax.while_loop(bis_cond, bis_body, (jnp.int32(0), jnp.int32(1), lo0, hi0))
    thr = _key_to_f32(hi)[:, :1]
    excess = jnp.where((thr > -jnp.inf) & (count(sc_scr[...] >= thr) > float(topk)), 1.0, 0.0)

    @pl.when(jnp.max(excess) > 0.0)
    def _ties():
        need = float(topk) - count(sc_scr[...] > thr)
        pos = lax.broadcasted_iota(I32, (R, L), 1)

        def pbis(_, lohi):
            lo, hi = lohi
            mid = (lo + hi) >> 1
            ok = count((sc_scr[...] == thr) & (pos <= mid)) >= need
            return jnp.where(ok, lo, mid), jnp.where(ok, mid, hi)

        nbits = int(math.ceil(math.log2(L))) + 1
        _, pcut = lax.fori_loop(0, nbits, pbis, (jnp.full((R, 1), -1, I32), jnp.full((R, 1), L - 1, I32)))
        pcut = jnp.where(excess > 0.0, pcut, L)
        s = sc_scr[...]
        sc_scr[...] = jnp.where((s == thr) & (pos > pcut), -jnp.inf, s)

    m_scr[...] = jnp.full(m_scr.shape, NEG, F32)
    l_scr[...] = jnp.zeros(l_scr.shape, F32)
    acc_scr[...] = jnp.zeros(acc_scr.shape, F32)
    q = q_ref[0]

    def softmax_step(h, logit, sel, vh):
        hs = slice(h * DSA_DH, (h + 1) * DSA_DH)
        logit = jnp.where(sel, logit, NEG)
        m_old = m_scr[h]
        m_new = jnp.maximum(m_old, jnp.max(logit, axis=1, keepdims=True))
        alpha = jnp.exp(m_old - m_new)
        p = jnp.exp(logit - m_new[:, :1])
        l_scr[h] = alpha * l_scr[h] + jnp.sum(p, axis=1, keepdims=True)
        acc_scr[:, hs] = alpha * acc_scr[:, hs] + _dot(p.astype(BF16), vh)
        m_scr[h] = m_new

    def attend_group(g, slot, last):
        c0 = pl.multiple_of(g * GK, GK)
        sel = sc_scr[:, pl.ds(c0, GK)] >= thr
        for h in range(DSA_HEADS):
            hs = slice(h * DSA_DH, (h + 1) * DSA_DH)
            logit = _dot_nt(q[:, hs], kbuf[slot, pl.ds(h, GK, stride=DSA_HEADS), :].astype(BF16))
            if last:
                bias = jnp.concatenate([jnp.full((R, GK - PS), far_ref[h], F32), tb_ref[0, h]], axis=1)
                logit = logit + bias
            else:
                logit = logit + far_ref[h]
            softmax_step(h, logit, sel, vbuf[slot, pl.ds(h, GK, stride=DSA_HEADS), :].astype(BF16))

    def group_body(g, c):
        slot = g % 2
        wait_group(g, slot)

        @pl.when(g + 1 < n_groups)
        def _():
            start_group(g + 1, 1 - slot)

        @pl.when(g < n_groups - 1)
        def _():
            attend_group(g, slot, False)

        @pl.when(g == n_groups - 1)
        def _():
            attend_group(g, slot, True)
        return c

    lax.fori_loop(0, n_groups, group_body, 0)

    sel_new = (sc_scr[:, past:L] >= thr) & new_ok
    kn = kn_ref[0]
    vn = vn_ref[0]
    for h in range(DSA_HEADS):
        hs = slice(h * DSA_DH, (h + 1) * DSA_DH)
        softmax_step(h, _dot_nt(q[:, hs], kn[:, hs]) + tb_ref[1, h], sel_new, vn[:, hs])

    outs = []
    for h in range(DSA_HEADS):
        hs = slice(h * DSA_DH, (h + 1) * DSA_DH)
        outs.append(acc_scr[:, hs] / l_scr[h])
    o_ref[0] = _rms(jnp.concatenate(outs, axis=1), g_ref[...]).astype(BF16)


def _dsa_sample(page_table, qi, wi, ki_new, q, k_new, v_new, cache_kidx, cache_k, cache_v, t5_bias, g_dsa,
                n_new, layer):
    Bd, n_pages = page_table.shape
    depth, n_phys, PS = cache_kidx.shape[:3]
    ckidx_t = jnp.swapaxes(cache_kidx, 2, 3)
    ck2 = cache_k.reshape(depth, n_phys, PS * DSA_HEADS, DSA_DH)
    cv2 = cache_v.reshape(depth, n_phys, PS * DSA_HEADS, DSA_DH)
    past = n_pages * PS
    R = DSA_S_ROWS
    topk = min(DSA_TOPK_MAX, (past + n_new) // 4)
    assert n_pages % DSA_S_GROUP == 0 and PS == LANE and n_new <= R
    t = np.arange(R)[:, None]
    c = np.arange(LANE)[None, :]
    tb = _t5_tiles(t5_bias, np.stack([PS + t - c, t - c]))
    far = t5_bias[REL_BUCKETS - 1]
    GK = DSA_S_GROUP * PS
    per_b = lambda shp: pl.BlockSpec((1,) + shp, lambda b, pt: (b,) + (0,) * len(shp))
    grid_spec = pltpu.PrefetchScalarGridSpec(
        num_scalar_prefetch=1,
        grid=(Bd,),
        in_specs=[per_b((IDX_HEADS * R, IDX_DH)), per_b((IDX_HEADS * R, 1)), per_b((LANE, IDX_DH)),
                  per_b((R, DSA_W)), per_b((LANE, DSA_W)), per_b((LANE, DSA_W)),
                  pl.BlockSpec((2, DSA_HEADS, R, LANE), lambda b, pt: (0, 0, 0, 0)),
                  pl.BlockSpec(memory_space=pltpu.SMEM),
                  pl.BlockSpec((1, DSA_W), lambda b, pt: (0, 0)),
                  pl.BlockSpec(memory_space=pl.ANY), pl.BlockSpec(memory_space=pl.ANY),
                  pl.BlockSpec(memory_space=pl.ANY)],
        out_specs=per_b((R, DSA_W)),
        scratch_shapes=[pltpu.VMEM((IDX_DH, past), F32), pltpu.VMEM((R, past + LANE), F32),
                        pltpu.VMEM((2, GK * DSA_HEADS, DSA_DH), F32), pltpu.VMEM((2, GK * DSA_HEADS, DSA_DH), F32),
                        pltpu.VMEM((DSA_HEADS, R, LANE), F32), pltpu.VMEM((DSA_HEADS, R, LANE), F32),
                        pltpu.VMEM((R, DSA_W), F32),
                        pltpu.SemaphoreType.DMA(()), pltpu.SemaphoreType.DMA((2,)),
                        pltpu.SemaphoreType.DMA((2,))],
    )
    return pl.pallas_call(
        functools.partial(_dsa_sample_kernel, topk=topk, n_new=n_new, layer=layer),
        grid_spec=grid_spec,
        out_shape=jax.ShapeDtypeStruct((Bd, R, DSA_W), BF16),
        compiler_params=_cparams(("arbitrary",)),
        name="dsa_s",
    )(page_table, qi, wi, ki_new, q, k_new, v_new, tb, far, g_dsa.reshape(1, DSA_W), ckidx_t, ck2, cv2)


def _post_kernel(og_ref, od_ref, x_ref, gt_ref, g2_ref, sc_ref, sh_ref, wo_ref, wpq_ref,
                 x1_ref, h2_ref, pq_ref):
    half = og_ref.shape[1]
    y = _dot(og_ref[...], wo_ref[:half, :]) + _dot(od_ref[...], wo_ref[half:, :])
    x1 = x_ref[...] + gt_ref[...] * y
    x1_ref[...] = x1
    h2 = (_rms(x1, g2_ref[...]) * (1.0 + sc_ref[...]) + sh_ref[...]).astype(BF16)
    h2_ref[...] = h2
    pq_ref[...] = _dot(h2, wpq_ref[...]).astype(BF16)


def _post(o_gla, o_dsa, x, gt1, g2, sc2, sh2, w_out, w_pq, tm):
    r, d = x.shape
    mod_spec = (pl.BlockSpec((1, d), lambda i: (0, 0)) if gt1.shape[0] == 1
                else pl.BlockSpec((tm, d), lambda i: (i, 0)))
    full = lambda shp: pl.BlockSpec(shp, lambda i: (0,) * len(shp))
    row = lambda w: pl.BlockSpec((tm, w), lambda i: (i, 0))
    return pl.pallas_call(
        _post_kernel,
        grid=(r // tm,),
        in_specs=[row(o_gla.shape[1]), row(o_dsa.shape[1]), row(d), mod_spec, full((1, d)), mod_spec, mod_spec,
                  full(w_out.shape), full(w_pq.shape)],
        out_specs=[row(d), row(d), row(w_pq.shape[1])],
        out_shape=[jax.ShapeDtypeStruct((r, d), F32), jax.ShapeDtypeStruct((r, d), BF16),
                   jax.ShapeDtypeStruct((r, w_pq.shape[1]), BF16)],
        compiler_params=_cparams(("arbitrary",)),
        name="post",
    )(o_gla, o_dsa, x, gt1, g2.reshape(1, d), sc2, sh2, w_out, w_pq)


PEER_NCAND = PEER_TOPK + 1


def _top_rows(work, n):
    vals = []
    for r in range(n):
        m = jnp.max(work, axis=0, keepdims=True)
        vals.append(m)
        if r + 1 < n:
            work = jnp.where(work == m, -jnp.inf, work)
    return vals


def _peer_sel_kernel(pq_ref, k1_ref, k2_ref, e1_ref, e2_ref, s2_ref, c_ref):
    tn = pq_ref.shape[0]
    n = PEER_NCAND
    npad = -(-n // SUBLANE) * SUBLANE
    for h in range(PEER_HEADS):
        pqh = pq_ref[:, h * PEER_DKEY:(h + 1) * PEER_DKEY]
        s1 = _dot_nt(k1_ref[h], pqh)
        s2 = _dot_nt(k2_ref[h], pqh)
        v1 = _top_rows(s1, n)
        v2 = _top_rows(s2, n)
        v2a = jnp.concatenate(v2 + [jnp.full((npad - n, tn), -jnp.inf, F32)], axis=0)
        cands = [v1[0] + v2a] + [v1[a] + v2a[:SUBLANE] for a in range(1, n)]
        assert all(n // (a + 1) <= SUBLANE for a in range(1, n))
        c = _top_rows(jnp.concatenate(cands, axis=0), n)
        thr = 0.5 * (c[PEER_TOPK - 1] + c[PEER_TOPK])
        z = jnp.ones_like(c[0])
        for r in range(1, PEER_TOPK):
            z = z + jnp.exp(c[r] - c[0])
        e1_ref[h] = jnp.exp(s1 - v1[0])
        e2_ref[h] = jnp.exp(s2 - v2[0]) / z
        s2_ref[h] = s2
        c_ref[h] = thr - s1


def _peer_sel(pq, k1p, k2p, tn):
    n, w = pq.shape
    blk = pl.BlockSpec((PEER_HEADS, PEER_NKEYS, tn), lambda i: (0, 0, i))
    full = lambda shp: pl.BlockSpec(shp, lambda i: (0,) * len(shp))
    shp = jax.ShapeDtypeStruct((PEER_HEADS, PEER_NKEYS, n), F32)
    return pl.pallas_call(
        _peer_sel_kernel,
        grid=(n // tn,),
        in_specs=[pl.BlockSpec((tn, w), lambda i: (i, 0)), full(k1p.shape), full(k2p.shape)],
        out_specs=[blk] * 4,
        out_shape=[shp] * 4,
        compiler_params=_cparams(("arbitrary",)),
        name="peer_sel",
    )(pq, k1p, k2p)


PEER_TN = 512
PEER_ROWS = 8


PEER_JC = 16


def _peer_kernel(h2_ref, e1_ref, e2_ref, s2_ref, c_ref, u_ref, vt_ref, x1_ref, gt_ref, gf_ref,
                 y_ref, acc_scr, a_scr, w_scr, *, final):
    j = pl.program_id(1)

    @pl.when(j == 0)
    def _():
        acc_scr[...] = jnp.zeros(acc_scr.shape, F32)

    a_scr[...] = _dot_nt(u_ref[...], h2_ref[...])
    for tb in range(a_scr.shape[1] // LANE):
        ts = slice(tb * LANE, (tb + 1) * LANE)

        def gate_block(jc, carry):
            j0 = pl.multiple_of(jc * PEER_JC, PEER_JC)
            gates = [jnp.zeros((PEER_JC, LANE), F32) for _ in range(PEER_ROWS)]
            for h in range(PEER_HEADS):
                s2 = s2_ref[h, pl.ds(j0, PEER_JC), ts]
                e2 = e2_ref[h, pl.ds(j0, PEER_JC), ts]
                for r in range(PEER_ROWS):
                    gates[r] = gates[r] + e1_ref[h, r:r + 1, ts] * jnp.where(s2 >= c_ref[h, r:r + 1, ts], e2, 0.0)
            for r in range(PEER_ROWS):
                rows = pl.ds(r * PEER_NKEYS + j0, PEER_JC)
                a = a_scr[rows, ts]
                gelu = 0.5 * a * (1.0 + lax.erf(a * (2.0 ** -0.5)))
                w_scr[rows, ts] = (gates[r] * gelu).astype(BF16)
            return carry

        lax.fori_loop(0, PEER_NKEYS // PEER_JC, gate_block, 0)
    acc_scr[...] += _dot(vt_ref[...], w_scr[...])

    @pl.when(j == pl.num_programs(1) - 1)
    def _():
        x2 = x1_ref[...] + gt_ref[...] * acc_scr[...].T
        y_ref[...] = _rms(x2, gf_ref[...]) if final else x2


def _peer(h2, e1, e2, s2, c, u_b, vt_b, x1, gt2, g_final, final):
    n, d = x1.shape
    tn = PEER_TN
    te = PEER_ROWS * PEER_NKEYS
    n_exp = u_b.shape[0]
    mod_spec = (pl.BlockSpec((1, d), lambda i, j: (0, 0)) if gt2.shape[0] == 1
                else pl.BlockSpec((tn, d), lambda i, j: (i, 0)))
    sel = pl.BlockSpec((PEER_HEADS, PEER_NKEYS, tn), lambda i, j: (0, 0, i))
    sel_rows = pl.BlockSpec((PEER_HEADS, PEER_ROWS, tn), lambda i, j: (0, j, i))
    row = pl.BlockSpec((tn, d), lambda i, j: (i, 0))
    return pl.pallas_call(
        functools.partial(_peer_kernel, final=final),
        grid=(n // tn, n_exp // te),
        in_specs=[row, sel_rows, sel, sel, sel_rows,
                  pl.BlockSpec((te, d), lambda i, j: (j, 0)),
                  pl.BlockSpec((d, te), lambda i, j: (0, j)),
                  row, mod_spec, pl.BlockSpec((1, d), lambda i, j: (0, 0))],
        out_specs=row,
        out_shape=jax.ShapeDtypeStruct((n, d), F32),
        scratch_shapes=[pltpu.VMEM((d, tn), F32), pltpu.VMEM((te, tn), F32), pltpu.VMEM((te, tn), BF16)],
        compiler_params=_cparams(("arbitrary", "arbitrary")),
        name="peer",
    )(h2, e1, e2, s2, c, u_b, vt_b, x1, gt2, g_final.reshape(1, d))


def _pad_in_proj(w_in):
    parts, o = [], 0
    for s, ps in zip(SPLIT_SIZES, _PAD_SIZES):
        parts.append(jnp.pad(w_in[:, o:o + s], ((0, 0), (0, ps - s))))
        o += s
    return jnp.concatenate(parts, axis=1).astype(BF16)


def _pad_peer_keys(keys, lo):
    half = keys.shape[-1]
    return jnp.pad(keys, ((0, 0), (0, 0), (lo, PEER_DKEY - half - lo))).astype(BF16)


def _pad_axis(a, axis, size):
    pad = [(0, 0)] * a.ndim
    pad[axis] = (0, size - a.shape[axis])
    return jnp.pad(a, pad)


def _rows_mod(mod, reps):
    parts = jnp.split(mod, 6, axis=-1)
    if mod.shape[0] == 1:
        return parts
    return [jnp.repeat(p, reps, axis=0) for p in parts]


def kernel(x_prompt, x_sample, cache_k, cache_v, cache_kidx, state_gla, page_table, c_prompt, c_sample,
           w_ada, b_ada, g_norm1, g_norm2, w_in, w_gate_a2, b_gate_a2, g_gla_out, g_dsa_out, w_out,
           t5_bias, w_peer_q, peer_keys1, peer_keys2, peer_u, peer_v, g_final):
    depth = w_ada.shape[0]
    B, T, D = x_prompt.shape
    Bd, Ts, _ = x_sample.shape
    xp = x_prompt.reshape(B * T, D)
    xs = x_sample.reshape(Bd * Ts, D)
    R = DSA_S_ROWS
    outs = [[] for _ in range(8)]
    for l in range(depth):
        w_in_p = _pad_in_proj(w_in[l])
        wa2_p = _pad_axis(w_gate_a2[l], 0, LANE).astype(BF16)
        w_out_b = w_out[l].astype(BF16)
        w_pq_b = w_peer_q[l].astype(BF16)
        k1p = _pad_peer_keys(peer_keys1[l], 0)
        k2p = _pad_peer_keys(peer_keys2[l], PEER_DKEY // 2)
        u_b = peer_u[l].astype(BF16)
        vt_b = peer_v[l].T.astype(BF16)
        last = l == depth - 1

        c_all = jnp.concatenate([c_prompt, c_sample], axis=0)
        c_all = _pad_axis(c_all, 0, -(-(B + Bd) // SUBLANE) * SUBLANE)
        mod = _ada(c_all, w_ada[l], b_ada[l])
        mod_p = _rows_mod(mod[:B], T)
        mod_s = _rows_mod(mod[B:B + Bd], Ts)

        sh1, sc1, gt1, sh2, sc2, gt2 = mod_p
        (gq, gk, gv, gg, lg, dq, dk, dv, dkb, dvb, qi, ki, kib, wi) = _proj(
            xp, g_norm1[l], sc1, sh1, w_in_p, wa2_p, b_gate_a2[l], 256)
        s0p = jnp.zeros((B, GLA_HEADS, GLA_DK, GLA_DV), F32)
        r3 = lambda a: a.reshape(B, T, a.shape[-1])
        o_gla, st_p = _gla(r3(gq), r3(gk), r3(gv), r3(gg), r3(lg), g_gla_out[l], s0p, min(GLA_CHUNK, T))
        seq = lambda a, b: a[b * T:(b + 1) * T]
        o_dsa = jnp.concatenate([
            _dsa_prompt(seq(qi, b).T, seq(wi, b)[:, :IDX_HEADS].T, seq(kib, b),
                        seq(dq, b).T, seq(dkb, b), seq(dvb, b).T, t5_bias, g_dsa_out[l])
            for b in range(B)], axis=0)
        x1, h2, pq = _post(o_gla.reshape(B * T, GLA_W), o_dsa, xp, gt1, g_norm2[l], sc2, sh2, w_out_b, w_pq_b, 256)
        e1, e2, s2, cth = _peer_sel(pq, k1p, k2p, 256)
        xp = _peer(h2, e1, e2, s2, cth, u_b, vt_b, x1, gt2, g_final, last)
        outs[0].append(dk.reshape(B, T, DSA_HEADS, DSA_DH))
        outs[1].append(dv.reshape(B, T, DSA_HEADS, DSA_DH))
        outs[2].append(ki.reshape(B, T, IDX_DH))
        outs[3].append(st_p)

        sh1, sc1, gt1, sh2, sc2, gt2 = mod_s
        (gq, gk, gv, gg, lg, dq, dk, dv, dkb, dvb, qi, ki, kib, wi) = _proj(
            xs, g_norm1[l], sc1, sh1, w_in_p, wa2_p, b_gate_a2[l], 256)
        tokpad = lambda a, n: _pad_axis(a.reshape(Bd, Ts, a.shape[-1]), 1, n)
        o_gla, st_s = _gla(tokpad(gq, R), tokpad(gk, R), tokpad(gv, R), tokpad(gg, R), tokpad(lg, R),
                           g_gla_out[l], state_gla[l], R)
        qi_s = _pad_axis(qi.reshape(Bd, Ts, IDX_HEADS, IDX_DH).transpose(0, 2, 1, 3), 2, R)
        wi_s = _pad_axis(wi[:, :IDX_HEADS].reshape(Bd, Ts, IDX_HEADS).transpose(0, 2, 1), 2, R)
        o_dsa = _dsa_sample(page_table, qi_s.reshape(Bd, IDX_HEADS * R, IDX_DH),
                            wi_s.reshape(Bd, IDX_HEADS * R, 1), tokpad(kib, LANE), tokpad(dq, R),
                            tokpad(dkb, LANE), tokpad(dvb, LANE),
                            cache_kidx, cache_k, cache_v, t5_bias, g_dsa_out[l], Ts, l)
        x1, h2, pq = _post(o_gla[:, :Ts].reshape(Bd * Ts, GLA_W), o_dsa[:, :Ts].reshape(Bd * Ts, DSA_W),
                           xs, gt1, g_norm2[l], sc2, sh2, w_out_b, w_pq_b, 256)
        e1, e2, s2, cth = _peer_sel(pq, k1p, k2p, 256)
        xs = _peer(h2, e1, e2, s2, cth, u_b, vt_b, x1, gt2, g_final, last)
        outs[4].append(dk.reshape(Bd, Ts, DSA_HEADS, DSA_DH))
        outs[5].append(dv.reshape(Bd, Ts, DSA_HEADS, DSA_DH))
        outs[6].append(ki.reshape(Bd, Ts, IDX_DH))
        outs[7].append(st_s)

    return (xp.reshape(B, T, D), xs.reshape(Bd, Ts, D)) + tuple(jnp.stack(o) for o in outs)
```

```python
import functools
import math

import numpy as np
import jax
import jax.numpy as jnp
from jax import lax
from jax.experimental import pallas as pl
from jax.experimental.pallas import tpu as pltpu

F32 = jnp.float32
BF16 = jnp.bfloat16
I32 = jnp.int32

EPS = 1e-6
GLA_HEADS, GLA_DK, GLA_DV = 4, 64, 128
GLA_GATE_RANK = 16
GLA_GATE_NORM = 16.0
GLA_QK = GLA_HEADS * GLA_DK
GLA_W = GLA_HEADS * GLA_DV
DSA_HEADS, DSA_DH = 4, 128
DSA_W = DSA_HEADS * DSA_DH
IDX_HEADS, IDX_DH = 8, 64
IDX_W = IDX_HEADS * IDX_DH
DSA_TOPK_MAX = 256
REL_BUCKETS, REL_MAX_DIST = 32, 128
PEER_HEADS, PEER_NKEYS, PEER_DKEY, PEER_TOPK = 8, 128, 128, 16
SPLIT_SIZES = (GLA_QK, GLA_QK, GLA_W, GLA_W, GLA_GATE_RANK,
               DSA_W, DSA_W, DSA_W, IDX_W, IDX_DH, IDX_HEADS)

LANE = 128
SUBLANE = 8
NEG = -1e30
VMEM_LIMIT = 56 * 1024 * 1024

_PAD_SIZES = tuple(-(-s // LANE) * LANE for s in SPLIT_SIZES)
_PAD_OFFS = tuple(int(v) for v in np.cumsum((0,) + _PAD_SIZES))
(O_GQ, O_GK, O_GV, O_GG, O_GA, O_DQ, O_DK, O_DV, O_QI, O_KI, O_WI, D_IN_PAD) = _PAD_OFFS


def _cparams(sem):
    return pltpu.CompilerParams(dimension_semantics=sem, vmem_limit_bytes=VMEM_LIMIT)


def _dot(a, b):
    return jnp.dot(a, b, preferred_element_type=F32)


def _dot_nt(a, b):
    return lax.dot_general(a, b, (((1,), (1,)), ((), ())), preferred_element_type=F32)


def _rms(x, g):
    return x * lax.rsqrt(jnp.mean(x * x, axis=-1, keepdims=True) + EPS) * g


def _ada_kernel(c_ref, w_ref, b_ref, o_ref):
    c = c_ref[...]
    s = c * jax.nn.sigmoid(c)
    o_ref[...] = _dot(s.astype(BF16), w_ref[...].astype(BF16)) + b_ref[...]


def _ada(c, w, b):
    m, d = c.shape
    n = w.shape[1]
    tn = 1024
    return pl.pallas_call(
        _ada_kernel,
        grid=(n // tn,),
        in_specs=[pl.BlockSpec((m, d), lambda j: (0, 0)),
                  pl.BlockSpec((d, tn), lambda j: (0, j)),
                  pl.BlockSpec((1, tn), lambda j: (0, j))],
        out_specs=pl.BlockSpec((m, tn), lambda j: (0, j)),
        out_shape=jax.ShapeDtypeStruct((m, n), F32),
        compiler_params=_cparams(("arbitrary",)),
        name="ada",
    )(c, w, b.reshape(1, n))


def _proj_kernel(x_ref, g_ref, sc_ref, sh_ref, w_ref, wa2_ref, ba2_ref,
                 q_ref, k_ref, v_ref, gg_ref, lg_ref, dq_ref, dk_ref, dv_ref,
                 dkb_ref, dvb_ref, qi_ref, ki_ref, kib_ref, wi_ref):
    x = x_ref[...]
    h = _rms(x, g_ref[...]) * (1.0 + sc_ref[...]) + sh_ref[...]
    p = _dot(h.astype(BF16), w_ref[...])
    q_ref[...] = p[:, O_GQ:O_GK] * (GLA_DK ** -0.5)
    k_ref[...] = p[:, O_GK:O_GV]
    v_ref[...] = p[:, O_GV:O_GG]
    gg_ref[...] = p[:, O_GG:O_GA]
    z = _dot(p[:, O_GA:O_DQ].astype(BF16), wa2_ref[...]) + ba2_ref[...]
    lg_ref[...] = (jnp.minimum(z, 0.0) - jnp.log(1.0 + jnp.exp(-jnp.abs(z)))) * (1.0 / GLA_GATE_NORM)
    dq_ref[...] = (p[:, O_DQ:O_DK] * (DSA_DH ** -0.5)).astype(BF16)
    dk = p[:, O_DK:O_DV]
    dv = p[:, O_DV:O_QI]
    dk_ref[...] = dk
    dv_ref[...] = dv
    dkb_ref[...] = dk.astype(BF16)
    dvb_ref[...] = dv.astype(BF16)
    qi_ref[...] = (p[:, O_QI:O_KI] * (IDX_DH ** -0.5)).astype(BF16)
    ki = p[:, O_KI:O_KI + IDX_DH]
    ki_ref[...] = ki
    kib_ref[...] = ki.astype(BF16)
    wi_ref[...] = p[:, O_WI:D_IN_PAD] * (IDX_HEADS ** -0.5)


def _proj(x, g1, sc, sh, w_in_p, wa2_p, ba2, tm):
    r, d = x.shape
    rows_mod = sc.shape[0]
    mod_spec = (pl.BlockSpec((1, d), lambda i: (0, 0)) if rows_mod == 1
                else pl.BlockSpec((tm, d), lambda i: (i, 0)))
    full = lambda shp: pl.BlockSpec(shp, lambda i: (0,) * len(shp))
    widths = [(GLA_QK, F32), (GLA_QK, F32), (GLA_W, F32), (GLA_W, F32), (GLA_QK, F32),
              (DSA_W, BF16), (DSA_W, F32), (DSA_W, F32), (DSA_W, BF16), (DSA_W, BF16),
              (IDX_W, BF16), (IDX_DH, F32), (IDX_DH, BF16), (LANE, F32)]
    return pl.pallas_call(
        _proj_kernel,
        grid=(r // tm,),
        in_specs=[pl.BlockSpec((tm, d), lambda i: (i, 0)), full((1, d)), mod_spec, mod_spec,
                  full(w_in_p.shape), full(wa2_p.shape), full((1, GLA_QK))],
        out_specs=[pl.BlockSpec((tm, w), lambda i: (i, 0)) for w, _ in widths],
        out_shape=[jax.ShapeDtypeStruct((r, w), dt) for w, dt in widths],
        compiler_params=_cparams(("arbitrary",)),
        name="proj",
    )(x, g1.reshape(1, d), sc, sh, w_in_p, wa2_p, ba2.reshape(1, GLA_QK))


GLA_SUB = 16
GLA_CHUNK = 128


def _gla_kernel(q_ref, k_ref, v_ref, gg_ref, lg_ref, go_ref, s0_ref, o_ref, s_ref, st_scr):
    c = pl.program_id(1)

    @pl.when(c == 0)
    def _():
        st_scr[...] = s0_ref[0]

    q = q_ref[0]
    k = k_ref[0]
    v = v_ref[0]
    lg = lg_ref[0]
    C = q.shape[0]
    rows = lax.broadcasted_iota(I32, (C, C), 0)
    cols = lax.broadcasted_iota(I32, (C, C), 1)
    tri = (rows >= cols).astype(F32)
    b = jnp.dot(tri, lg, preferred_element_type=F32, precision=lax.Precision.HIGHEST)
    b_last = b[C - 1:C, :]
    qc = q * jnp.exp(b)
    kc = k * jnp.exp(b_last - b)
    kcT = kc.T
    dec_col = jnp.broadcast_to(jnp.exp(jnp.sum(lg.T, axis=1, keepdims=True)), (GLA_QK, GLA_DV))
    go = go_ref[...]
    outs = []
    for h in range(GLA_HEADS):
        sl = slice(h * GLA_DK, (h + 1) * GLA_DK)
        vh = v[:, h * GLA_DV:(h + 1) * GLA_DV].astype(BF16)
        S = st_scr[h]
        o = _dot(qc[:, sl].astype(BF16), S.astype(BF16))
        o_rows = []
        sub = min(GLA_SUB, C)
        for blk in range(C // sub):
            r0, r1 = blk * sub, (blk + 1) * sub
            bref = b[r0 - 1:r0, sl] if blk > 0 else jnp.zeros((1, GLA_DK), F32)
            qt = q[r0:r1, sl] * jnp.exp(b[r0:r1, sl] - bref)
            kt = k[:r1, sl] * jnp.exp(bref - b[:r1, sl])
            a = _dot_nt(qt.astype(BF16), kt.astype(BF16))
            causal = (lax.broadcasted_iota(I32, (sub, r1), 1)
                      <= lax.broadcasted_iota(I32, (sub, r1), 0) + r0)
            a = jnp.where(causal, a, 0.0)
            o_rows.append(_dot(a.astype(BF16), vh[:r1]))
        o = o + (jnp.concatenate(o_rows, axis=0) if len(o_rows) > 1 else o_rows[0])
        st_scr[h] = dec_col[sl] * S + _dot(kcT[sl].astype(BF16), vh)
        outs.append(_rms(o, go))
    o_all = jnp.concatenate(outs, axis=1)
    gg = gg_ref[0]
    o_ref[0] = (o_all * (gg * jax.nn.sigmoid(gg))).astype(BF16)

    @pl.when(c == pl.num_programs(1) - 1)
    def _():
        s_ref[0] = st_scr[...]


def _gla(q, k, v, gg, lg, g_out, s0, chunk):
    B, T, _ = q.shape
    n = T // chunk
    tok = lambda w: pl.BlockSpec((1, chunk, w), lambda b, c: (b, c, 0))
    st = pl.BlockSpec((1, GLA_HEADS, GLA_DK, GLA_DV), lambda b, c: (b, 0, 0, 0))
    return pl.pallas_call(
        _gla_kernel,
        grid=(B, n),
        in_specs=[tok(GLA_QK), tok(GLA_QK), tok(GLA_W), tok(GLA_W), tok(GLA_QK),
                  pl.BlockSpec((1, GLA_DV), lambda b, c: (0, 0)), st],
        out_specs=[tok(GLA_W), st],
        out_shape=[jax.ShapeDtypeStruct((B, T, GLA_W), BF16),
                   jax.ShapeDtypeStruct((B, GLA_HEADS, GLA_DK, GLA_DV), F32)],
        scratch_shapes=[pltpu.VMEM((GLA_HEADS, GLA_DK, GLA_DV), F32)],
        compiler_params=_cparams(("arbitrary", "arbitrary")),
        name="gla",
    )(q, k, v, gg, lg, g_out.reshape(1, GLA_DV), s0)


def _t5_bucket_np(d):
    max_exact = REL_BUCKETS // 2
    d = np.maximum(d, 0)
    large = max_exact + (np.log(np.maximum(d, 1).astype(np.float32) / np.float32(max_exact))
                         / np.float32(math.log(REL_MAX_DIST / max_exact))
                         * np.float32(REL_BUCKETS - max_exact)).astype(np.int32)
    large = np.minimum(large, REL_BUCKETS - 1)
    return np.where(d < max_exact, d, large).astype(np.int32)


T5_FAR = 113
assert int(_t5_bucket_np(np.arange(T5_FAR, 1 << 16)).min()) == REL_BUCKETS - 1


def _t5_kernel(bias_ref, idx_ref, o_ref):
    idx = idx_ref[0]
    for h in range(DSA_HEADS):
        acc = jnp.zeros(idx.shape, F32)
        for bkt in range(REL_BUCKETS):
            acc = jnp.where(idx == bkt, bias_ref[bkt, h], acc)
        o_ref[0, h] = acc - bias_ref[REL_BUCKETS - 1, h]


def _t5_tiles(t5_bias, dist_tiles):
    idx = jnp.asarray(_t5_bucket_np(dist_tiles))
    n, r, c = idx.shape
    return pl.pallas_call(
        _t5_kernel,
        grid=(n,),
        in_specs=[pl.BlockSpec(memory_space=pltpu.SMEM),
                  pl.BlockSpec((1, r, c), lambda i: (i, 0, 0))],
        out_specs=pl.BlockSpec((1, DSA_HEADS, r, c), lambda i: (i, 0, 0, 0)),
        out_shape=jax.ShapeDtypeStruct((n, DSA_HEADS, r, c), F32),
        compiler_params=_cparams(("arbitrary",)),
        name="t5tiles",
    )(t5_bias, idx)


def _key_to_f32(m):
    bits = m ^ ((m >> 31) & jnp.int32(0x7FFFFFFF))
    return pltpu.bitcast(bits, F32)


KEY_NEG_INF = int(np.int32(-2139095041))
KEY_POS_INF = 0x7F800000


def _avg_floor(lo, hi):
    return (lo >> 1) + (hi >> 1) + (lo & hi & 1)


DSA_QB = 256
DSA_KV_STEP = 2048


def _dsa_prompt_kernel(qiT_ref, wiT_ref, ki_ref, qT_ref, k_ref, vT_ref, tb_ref, g_ref,
                       o_ref, sc_scr, thr_scr, m_scr, l_scr, acc_scr, *, topk, n_sub):
    i = pl.program_id(0)
    j = pl.program_id(1)
    QB = DSA_QB
    NV = QB // SUBLANE
    ntile = i + 1
    shape3 = (NV, SUBLANE, QB)
    key3 = (lax.broadcasted_iota(I32, shape3, 0) * SUBLANE + lax.broadcasted_iota(I32, shape3, 1))
    qry3 = lax.broadcasted_iota(I32, shape3, 2)

    def tile3(t):
        c0 = pl.multiple_of(t * QB, QB)
        return sc_scr[pl.ds(c0, QB), :].reshape(shape3)

    def count(pred3_of_tile):
        def body(t, cnt8):
            return cnt8 + jnp.sum(jnp.where(pred3_of_tile(t), 1.0, 0.0), axis=0)
        cnt8 = lax.fori_loop(0, ntile, body, jnp.zeros((SUBLANE, QB), F32))
        return jnp.sum(cnt8, axis=0, keepdims=True)

    @pl.when(j == 0)
    def _scores_and_threshold():
        wiT = wiT_ref[...]

        def score_tile(t, carry):
            c0 = pl.multiple_of(t * QB, QB)
            kt = ki_ref[pl.ds(c0, QB), :]
            acc = jnp.zeros((QB, QB), F32)
            for h in range(IDX_HEADS):
                s = _dot(kt, qiT_ref[h * IDX_DH:(h + 1) * IDX_DH, :])
                acc = acc + jnp.maximum(s, 0.0) * wiT[h:h + 1, :]
            acc3 = jnp.where((t < i) | (key3 <= qry3), acc.reshape(shape3), -jnp.inf)
            sc_scr[pl.ds(c0, QB), :] = acc3.reshape(QB, QB)
            return carry

        lax.fori_loop(0, ntile, score_tile, 0)

        def bis_cond(c):
            it, n_open, _, _ = c
            return (it < 32) & (n_open > 0)

        def bis_body(c):
            it, _, lo, hi = c
            mid = _avg_floor(lo, hi)
            midf = _key_to_f32(mid)
            cnt = count(lambda t: tile3(t) > midf[None])
            is_open = mid > lo
            ok = cnt <= float(topk - 1)
            exact = cnt == float(topk)
            lo2 = jnp.where(is_open & ~ok, mid, lo)
            hi2 = jnp.where(is_open & ok, mid, jnp.where(is_open & exact, mid + 1, hi))
            n_open = jnp.max(jnp.where(_avg_floor(lo2, hi2) > lo2, 1, 0))
            return it + 1, n_open, lo2, hi2

        lo0 = jnp.full((SUBLANE, QB), KEY_NEG_INF - 1, I32)
        hi0 = jnp.full((SUBLANE, QB), KEY_POS_INF, I32)
        _, _, _, hi = lax.while_loop(bis_cond, bis_body, (jnp.int32(0), jnp.int32(1), lo0, hi0))
        thr8 = _key_to_f32(hi)
        n_ge = count(lambda t: tile3(t) >= thr8[None])

        def short_of(thr, n):
            return jnp.max(jnp.where((n < float(topk)) & (thr[:1] > -jnp.inf), 1, 0))

        def lower_cond(c):
            it, n_short, _, _ = c
            return (it < 8) & (n_short > 0)

        def lower_body(c):
            it, _, thr, n = c

            def below(t, b8):
                s3 = tile3(t)
                return jnp.maximum(b8, jnp.max(jnp.where(s3 < thr[None], s3, -jnp.inf), axis=0))

            b8 = lax.fori_loop(0, ntile, below, jnp.full((SUBLANE, QB), -jnp.inf, F32))
            nxt = jnp.max(b8, axis=0, keepdims=True)
            thr2 = jnp.where((n < float(topk)) & (thr[:1] > -jnp.inf), jnp.broadcast_to(nxt, thr.shape), thr)
            n2 = count(lambda t: tile3(t) >= thr2[None])
            return it + 1, short_of(thr2, n2), thr2, n2

        _, _, thr8, n_ge = lax.while_loop(lower_cond, lower_body, (jnp.int32(0), short_of(thr8, n_ge), thr8, n_ge))
        thr_scr[...] = thr8

        excess = jnp.where((thr8[:1] > -jnp.inf) & (n_ge > float(topk)), 1.0, 0.0)

        @pl.when(jnp.max(excess) > 0.0)
        def _ties():
            need = float(topk) - count(lambda t: tile3(t) > thr8[None])

            def pbis(_, lohi):
                lo, hi = lohi
                mid = (lo + hi) >> 1
                n = count(lambda t: (tile3(t) == thr8[None]) & (key3 + t * QB <= mid[None]))
                ok = n >= need
                return jnp.where(ok, lo, mid), jnp.where(ok, mid, hi)

            n_keys = sc_scr.shape[0]
            nbits = int(math.ceil(math.log2(n_keys))) + 1
            plo = jnp.full((SUBLANE, QB), -1, I32)
            phi = jnp.full((SUBLANE, QB), n_keys - 1, I32)
            _, pcut = lax.fori_loop(0, nbits, pbis, (plo, phi))
            pcut = jnp.where(excess > 0.0, pcut, n_keys)

            def drop(t, carry):
                c0 = pl.multiple_of(t * QB, QB)
                s3 = tile3(t)
                s3 = jnp.where((s3 == thr8[None]) & (key3 + t * QB > pcut[None]), -jnp.inf, s3)
                sc_scr[pl.ds(c0, QB), :] = s3.reshape(QB, QB)
                return carry

            lax.fori_loop(0, ntile, drop, 0)

        def around(t, c):
            a8, b8 = c
            s3 = tile3(t)
            ge = s3 >= thr8[None]
            a8 = jnp.minimum(a8, jnp.min(jnp.where(ge, s3, jnp.inf), axis=0))
            b8 = jnp.maximum(b8, jnp.max(jnp.where(ge, -jnp.inf, s3), axis=0))
            return a8, b8

        a8, b8 = lax.fori_loop(0, ntile, around, (jnp.full((SUBLANE, QB), jnp.inf, F32),
                                                  jnp.full((SUBLANE, QB), -jnp.inf, F32)))
        a = jnp.min(a8, axis=0, keepdims=True)
        bmax = jnp.max(b8, axis=0, keepdims=True)
        mid = 0.5 * (a + bmax)
        centred = jnp.where(bmax > -jnp.inf, jnp.where(mid > bmax, mid, a), thr8[:1])
        thr_scr[...] = jnp.broadcast_to(centred, (SUBLANE, QB))

        m_scr[...] = jnp.full(m_scr.shape, NEG, F32)
        l_scr[...] = jnp.zeros(l_scr.shape, F32)
        acc_scr[...] = jnp.zeros(acc_scr.shape, F32)

    thr8 = thr_scr[...]

    def attend(subs, kind):
        r0s = [pl.multiple_of(s * QB, QB) for s in subs]
        sels = [tile3(j * n_sub + s) >= thr8[None] for s in subs]
        if kind == 2:
            sels = [sel & (key3 <= qry3) for sel in sels]
        for h in range(DSA_HEADS):
            hs = slice(h * DSA_DH, (h + 1) * DSA_DH)
            l3s = []
            for r0, sel in zip(r0s, sels):
                logit = _dot(k_ref[pl.ds(r0, QB), hs], qT_ref[hs, :])
                if kind != 0:
                    logit = logit + tb_ref[kind - 1, h]
                l3s.append(jnp.where(sel, logit.reshape(shape3), NEG))
            m_old = m_scr[h]
            cmax = functools.reduce(jnp.maximum, [jnp.max(l3, axis=0) for l3 in l3s])
            m_new = jnp.maximum(m_old, jnp.max(cmax, axis=0, keepdims=True))
            alpha = jnp.exp(m_old - m_new)
            p3s = [jnp.exp(l3 - m_new[None]) for l3 in l3s]
            l_scr[h] = alpha * l_scr[h] + sum(jnp.sum(p3, axis=0) for p3 in p3s)
            pv = sum(_dot(vT_ref[hs, pl.ds(r0, QB)], p3.reshape(QB, QB).astype(BF16))
                     for r0, p3 in zip(r0s, p3s))
            acc3 = acc_scr[hs, :].reshape(DSA_DH // SUBLANE, SUBLANE, QB) * alpha[None]
            acc_scr[hs, :] = acc3.reshape(DSA_DH, QB) + pv
            m_scr[h] = m_new

    t0 = j * n_sub
    n_far = jnp.clip(i - 1 - t0, 0, n_sub)

    def far_quad(s, carry):
        attend([4 * s, 4 * s + 1, 4 * s + 2, 4 * s + 3], 0)
        return carry

    @pl.when(n_far == 8)
    def _far_all():
        attend(list(range(8)), 0)

    lax.fori_loop(0, jnp.where(n_far == 8, 0, n_far // 4), far_quad, 0)
    n_done = (n_far // 4) * 4

    @pl.when(n_far - n_done >= 2)
    def _far_pair():
        attend([n_done, n_done + 1], 0)

    @pl.when(n_far % 2 == 1)
    def _far_odd():
        attend([n_far - 1], 0)

    @pl.when((i - 1 >= t0) & (i - 1 < t0 + n_sub))
    def _prev():
        attend([i - 1 - t0], 1)

    @pl.when((i >= t0) & (i < t0 + n_sub))
    def _diag():
        attend([i - t0], 2)

    @pl.when(j == pl.num_programs(1) - 1)
    def _finish():
        outs = []
        for h in range(DSA_HEADS):
            hs = slice(h * DSA_DH, (h + 1) * DSA_DH)
            l_tot = jnp.sum(l_scr[h], axis=0, keepdims=True)
            outs.append(acc_scr[hs, :] / l_tot)
        oT = jnp.concatenate(outs, axis=0)
        inv = lax.rsqrt(jnp.mean(oT * oT, axis=0, keepdims=True) + EPS)
        gcol = jnp.concatenate([g_ref[...]] * (QB // LANE), axis=1)
        o_ref[...] = (oT * inv * gcol).T.astype(BF16)


def _dsa_prompt(qiT, wiT, ki, qT, k, vT, t5_bias, g_dsa):
    T = k.shape[0]
    QB = DSA_QB
    topk = min(DSA_TOPK_MAX, T // 4)
    kv_step = min(DSA_KV_STEP, T)
    assert T % QB == 0 and T % kv_step == 0 and kv_step % QB == 0
    nq, nkv = T // QB, T // kv_step
    kk = np.arange(QB)[:, None]
    qq = np.arange(QB)[None, :]
    tb = _t5_tiles(t5_bias, np.stack([QB + qq - kk, qq - kk]))
    last_kv = lambda i: ((i + 1) * QB - 1) // kv_step
    kv_idx = lambda i, j: jnp.minimum(j, last_kv(i))
    qcol = lambda r: pl.BlockSpec((r, QB), lambda i, j: (0, i))
    g_rep = jnp.broadcast_to(g_dsa.reshape(DSA_W, 1), (DSA_W, LANE))
    return pl.pallas_call(
        functools.partial(_dsa_prompt_kernel, topk=topk, n_sub=kv_step // QB),
        grid=(nq, nkv),
        in_specs=[qcol(IDX_W), qcol(IDX_HEADS), pl.BlockSpec((T, IDX_DH), lambda i, j: (0, 0)),
                  qcol(DSA_W),
                  pl.BlockSpec((kv_step, DSA_W), lambda i, j: (kv_idx(i, j), 0)),
                  pl.BlockSpec((DSA_W, kv_step), lambda i, j: (0, kv_idx(i, j))),
                  pl.BlockSpec((2, DSA_HEADS, QB, QB), lambda i, j: (0, 0, 0, 0)),
                  pl.BlockSpec((DSA_W, LANE), lambda i, j: (0, 0))],
        out_specs=pl.BlockSpec((QB, DSA_W), lambda i, j: (i, 0)),
        out_shape=jax.ShapeDtypeStruct((T, DSA_W), BF16),
        scratch_shapes=[pltpu.VMEM((T, QB), F32), pltpu.VMEM((SUBLANE, QB), F32),
                        pltpu.VMEM((DSA_HEADS, SUBLANE, QB), F32), pltpu.VMEM((DSA_HEADS, SUBLANE, QB), F32),
                        pltpu.VMEM((DSA_W, QB), F32)],
        compiler_params=_cparams(("arbitrary", "arbitrary")),
        name="dsa_p",
    )(qiT, wiT, ki, qT, k, vT, tb, g_rep)


DSA_S_ROWS = 8
DSA_S_GROUP = 16


def _dsa_sample_kernel(pt_ref, qi_ref, wi_ref, kin_ref, q_ref, kn_ref, vn_ref, tb_ref, g_ref,
                       ckidx_hbm, ck_hbm, cv_hbm, o_ref,
                       kidx_buf, sc_scr, kbuf, vbuf, m_scr, l_scr, acc_scr, sem_i, sem_k, sem_v,
                       *, topk, n_new, layer):
    b = pl.program_id(0)
    nb = pl.num_programs(0)
    n_pages = pt_ref.shape[1]
    PS = kidx_buf.shape[2] // n_pages
    past = n_pages * PS
    R = DSA_S_ROWS
    G = DSA_S_GROUP
    GK = G * PS
    n_groups = n_pages // G
    L = past + LANE
    PR = PS * DSA_HEADS
    islot = b % 2

    def kidx_copy(bb, p):
        return pltpu.make_async_copy(ckidx_hbm.at[layer, pt_ref[bb, p]],
                                     kidx_buf.at[bb % 2, :, pl.ds(p * PS, PS)], sem_i.at[bb % 2])

    def kv_copies(bb, g, slot, p):
        page = pt_ref[bb, g * G + p]
        return (pltpu.make_async_copy(ck_hbm.at[layer, page], kbuf.at[slot, pl.ds(p * PR, PR)], sem_k.at[slot]),
                pltpu.make_async_copy(cv_hbm.at[layer, page], vbuf.at[slot, pl.ds(p * PR, PR)], sem_v.at[slot]))

    def start_group(bb, g, slot):
        def body(p, c):
            ck, cv = kv_copies(bb, g, slot, p)
            ck.start()
            cv.start()
            return c
        lax.fori_loop(0, G, body, 0)

    def wait_group(bb, g, slot):
        def body(p, c):
            ck, cv = kv_copies(bb, g, slot, p)
            ck.wait()
            cv.wait()
            return c
        lax.fori_loop(0, G, body, 0)

    def start_kidx(bb):
        def body(p, c):
            kidx_copy(bb, p).start()
            return c
        lax.fori_loop(0, n_pages, body, 0)

    def wait_kidx(bb):
        def body(p, c):
            kidx_copy(bb, p).wait()
            return c
        lax.fori_loop(0, n_pages, body, 0)

    @pl.when(b == 0)
    def _():
        start_kidx(b)
        start_group(b, 0, 0)

    wait_kidx(b)

    @pl.when(b + 1 < nb)
    def _():
        start_kidx(b + 1)

    qi = qi_ref[0]
    wi = wi_ref[0]

    def head_sum(s):
        r = jnp.maximum(s, 0.0) * wi
        tot = r[0:R]
        for h in range(1, IDX_HEADS):
            tot = tot + r[h * R:(h + 1) * R]
        return tot

    def score_chunk(cidx, c):
        c0 = pl.multiple_of(cidx * GK, GK)
        kc = kidx_buf[islot, :, pl.ds(c0, GK)].astype(BF16)
        sc_scr[:, pl.ds(c0, GK)] = head_sum(_dot(qi, kc))
        return c

    lax.fori_loop(0, n_groups, score_chunk, 0)
    trow = lax.broadcasted_iota(I32, (R, LANE), 0)
    tcol = lax.broadcasted_iota(I32, (R, LANE), 1)
    new_ok = (tcol <= trow) & (tcol < n_new)
    sc_scr[:, past:L] = jnp.where(new_ok, head_sum(_dot_nt(qi, kin_ref[0])), -jnp.inf)

    def count(pred):
        return jnp.sum(jnp.where(pred, 1.0, 0.0), axis=1, keepdims=True)

    def bis_cond(c):
        it, n_open, _, _ = c
        return (it < 32) & (n_open > 0)

    def bis_body(c):
        it, _, lo, hi = c
        mid = _avg_floor(lo, hi)
        cnt = count(sc_scr[...] > _key_to_f32(mid)[:, :1])
        is_open = mid > lo
        ok = cnt <= float(topk - 1)
        exact = cnt == float(topk)
        lo2 = jnp.where(is_open & ~ok, mid, lo)
        hi2 = jnp.where(is_open & ok, mid, jnp.where(is_open & exact, mid + 1, hi))
        n_open = jnp.max(jnp.where(_avg_floor(lo2, hi2) > lo2, 1, 0))
        return it + 1, n_open, lo2, hi2

    lo0 = jnp.full((R, LANE), KEY_NEG_INF - 1, I32)
    hi0 = jnp.full((R, LANE), KEY_POS_INF, I32)
    _, _, _, hi = lax.while_loop(bis_cond, bis_body, (jnp.int32(0), jnp.int32(1), lo0, hi0))
    thr = _key_to_f32(hi)[:, :1]
    n_ge = count(sc_scr[...] >= thr)

    def short_of(t, n):
        return jnp.max(jnp.where((n < float(topk)) & (t > -jnp.inf), 1, 0))

    def lower_cond(c):
        it, n_short, _, _ = c
        return (it < 8) & (n_short > 0)

    def lower_body(c):
        it, _, t, n = c
        s = sc_scr[...]
        nxt = jnp.max(jnp.where(s < t, s, -jnp.inf), axis=1, keepdims=True)
        t2 = jnp.where((n < float(topk)) & (t > -jnp.inf), nxt, t)
        n2 = count(sc_scr[...] >= t2)
        return it + 1, short_of(t2, n2), t2, n2

    _, _, thr, n_ge = lax.while_loop(lower_cond, lower_body, (jnp.int32(0), short_of(thr, n_ge), thr, n_ge))
    excess = jnp.where((thr > -jnp.inf) & (n_ge > float(topk)), 1.0, 0.0)

    @pl.when(jnp.max(excess) > 0.0)
    def _ties():
        need = float(topk) - count(sc_scr[...] > thr)
        pos = lax.broadcasted_iota(I32, (R, L), 1)

        def pbis(_, lohi):
            lo, hi = lohi
            mid = (lo + hi) >> 1
            ok = count((sc_scr[...] == thr) & (pos <= mid)) >= need
            return jnp.where(ok, lo, mid), jnp.where(ok, mid, hi)

        nbits = int(math.ceil(math.log2(L))) + 1
        _, pcut = lax.fori_loop(0, nbits, pbis, (jnp.full((R, 1), -1, I32), jnp.full((R, 1), L - 1, I32)))
        pcut = jnp.where(excess > 0.0, pcut, L)
        s = sc_scr[...]
        sc_scr[...] = jnp.where((s == thr) & (pos > pcut), -jnp.inf, s)

    s_all = sc_scr[...]
    ge = s_all >= thr
    a = jnp.min(jnp.where(ge, s_all, jnp.inf), axis=1, keepdims=True)
    bmax = jnp.max(jnp.where(ge, -jnp.inf, s_all), axis=1, keepdims=True)
    mid = 0.5 * (a + bmax)
    thr = jnp.where(bmax > -jnp.inf, jnp.where(mid > bmax, mid, a), thr)

    m_scr[...] = jnp.full(m_scr.shape, NEG, F32)
    l_scr[...] = jnp.zeros(l_scr.shape, F32)
    acc_scr[...] = jnp.zeros(acc_scr.shape, F32)
    q = q_ref[0]

    def softmax_step(h, logit, sel, vh):
        hs = slice(h * DSA_DH, (h + 1) * DSA_DH)
        logit = jnp.where(sel, logit, NEG)
        m_old = m_scr[h]
        m_new = jnp.maximum(m_old, jnp.max(logit, axis=1, keepdims=True))
        alpha = jnp.exp(m_old - m_new)
        p = jnp.exp(logit - m_new[:, :1])
        l_scr[h] = alpha * l_scr[h] + jnp.sum(p, axis=1, keepdims=True)
        acc_scr[:, hs] = alpha * acc_scr[:, hs] + _dot(p.astype(BF16), vh)
        m_scr[h] = m_new

    def attend_group(g, slot, last):
        c0 = pl.multiple_of(g * GK, GK)
        sel = sc_scr[:, pl.ds(c0, GK)] >= thr
        for h in range(DSA_HEADS):
            hs = slice(h * DSA_DH, (h + 1) * DSA_DH)
            logit = _dot_nt(q[:, hs], kbuf[slot, pl.ds(h, GK, stride=DSA_HEADS), :].astype(BF16))
            if last:
                logit = logit + jnp.concatenate([jnp.zeros((R, GK - PS), F32), tb_ref[0, h]], axis=1)
            softmax_step(h, logit, sel, vbuf[slot, pl.ds(h, GK, stride=DSA_HEADS), :].astype(BF16))

    def group_body(g, c):
        slot = g % 2
        wait_group(b, g, slot)

        @pl.when(g + 1 < n_groups)
        def _():
            start_group(b, g + 1, 1 - slot)

        @pl.when((g + 1 == n_groups) & (b + 1 < nb))
        def _():
            start_group(b + 1, 0, 0)

        @pl.when(g < n_groups - 1)
        def _():
            attend_group(g, slot, False)

        @pl.when(g == n_groups - 1)
        def _():
            attend_group(g, slot, True)
        return c

    lax.fori_loop(0, n_groups, group_body, 0)

    sel_new = (sc_scr[:, past:L] >= thr) & new_ok
    kn = kn_ref[0]
    vn = vn_ref[0]
    for h in range(DSA_HEADS):
        hs = slice(h * DSA_DH, (h + 1) * DSA_DH)
        softmax_step(h, _dot_nt(q[:, hs], kn[:, hs]) + tb_ref[1, h], sel_new, vn[:, hs])

    outs = []
    for h in range(DSA_HEADS):
        hs = slice(h * DSA_DH, (h + 1) * DSA_DH)
        outs.append(acc_scr[:, hs] / l_scr[h])
    o_ref[0] = _rms(jnp.concatenate(outs, axis=1), g_ref[...]).astype(BF16)


def _dsa_sample(page_table, qi, wi, ki_new, q, k_new, v_new, cache_kidx, cache_k, cache_v, t5_bias, g_dsa,
                n_new, layer):
    Bd, n_pages = page_table.shape
    depth, n_phys, PS = cache_kidx.shape[:3]
    ckidx_t = jnp.swapaxes(cache_kidx, 2, 3)
    ck2 = cache_k.reshape(depth, n_phys, PS * DSA_HEADS, DSA_DH)
    cv2 = cache_v.reshape(depth, n_phys, PS * DSA_HEADS, DSA_DH)
    past = n_pages * PS
    R = DSA_S_ROWS
    topk = min(DSA_TOPK_MAX, (past + n_new) // 4)
    assert n_pages % (2 * DSA_S_GROUP) == 0 and PS == LANE and n_new <= R
    t = np.arange(R)[:, None]
    c = np.arange(LANE)[None, :]
    tb = _t5_tiles(t5_bias, np.stack([PS + t - c, t - c]))
    GK = DSA_S_GROUP * PS
    per_b = lambda shp: pl.BlockSpec((1,) + shp, lambda b, pt: (b,) + (0,) * len(shp))
    grid_spec = pltpu.PrefetchScalarGridSpec(
        num_scalar_prefetch=1,
        grid=(Bd,),
        in_specs=[per_b((IDX_HEADS * R, IDX_DH)), per_b((IDX_HEADS * R, 1)), per_b((LANE, IDX_DH)),
                  per_b((R, DSA_W)), per_b((LANE, DSA_W)), per_b((LANE, DSA_W)),
                  pl.BlockSpec((2, DSA_HEADS, R, LANE), lambda b, pt: (0, 0, 0, 0)),
                  pl.BlockSpec((1, DSA_W), lambda b, pt: (0, 0)),
                  pl.BlockSpec(memory_space=pl.ANY), pl.BlockSpec(memory_space=pl.ANY),
                  pl.BlockSpec(memory_space=pl.ANY)],
        out_specs=per_b((R, DSA_W)),
        scratch_shapes=[pltpu.VMEM((2, IDX_DH, past), F32), pltpu.VMEM((R, past + LANE), F32),
                        pltpu.VMEM((2, GK * DSA_HEADS, DSA_DH), F32), pltpu.VMEM((2, GK * DSA_HEADS, DSA_DH), F32),
                        pltpu.VMEM((DSA_HEADS, R, LANE), F32), pltpu.VMEM((DSA_HEADS, R, LANE), F32),
                        pltpu.VMEM((R, DSA_W), F32),
                        pltpu.SemaphoreType.DMA((2,)), pltpu.SemaphoreType.DMA((2,)),
                        pltpu.SemaphoreType.DMA((2,))],
    )
    return pl.pallas_call(
        functools.partial(_dsa_sample_kernel, topk=topk, n_new=n_new, layer=layer),
        grid_spec=grid_spec,
        out_shape=jax.ShapeDtypeStruct((Bd, R, DSA_W), BF16),
        compiler_params=_cparams(("arbitrary",)),
        name="dsa_s",
    )(page_table, qi, wi, ki_new, q, k_new, v_new, tb, g_dsa.reshape(1, DSA_W), ckidx_t, ck2, cv2)


def _post_kernel(og_ref, od_ref, x_ref, gt_ref, g2_ref, sc_ref, sh_ref, wo_ref, wpq_ref,
                 x1_ref, h2_ref, pq_ref):
    half = og_ref.shape[1]
    y = _dot(og_ref[...], wo_ref[:half, :]) + _dot(od_ref[...], wo_ref[half:, :])
    x1 = x_ref[...] + gt_ref[...] * y
    x1_ref[...] = x1
    h2 = (_rms(x1, g2_ref[...]) * (1.0 + sc_ref[...]) + sh_ref[...]).astype(BF16)
    h2_ref[...] = h2
    pq_ref[...] = _dot(h2, wpq_ref[...]).astype(BF16)


def _post(o_gla, o_dsa, x, gt1, g2, sc2, sh2, w_out, w_pq, tm):
    r, d = x.shape
    mod_spec = (pl.BlockSpec((1, d), lambda i: (0, 0)) if gt1.shape[0] == 1
                else pl.BlockSpec((tm, d), lambda i: (i, 0)))
    full = lambda shp: pl.BlockSpec(shp, lambda i: (0,) * len(shp))
    row = lambda w: pl.BlockSpec((tm, w), lambda i: (i, 0))
    return pl.pallas_call(
        _post_kernel,
        grid=(r // tm,),
        in_specs=[row(o_gla.shape[1]), row(o_dsa.shape[1]), row(d), mod_spec, full((1, d)), mod_spec, mod_spec,
                  full(w_out.shape), full(w_pq.shape)],
        out_specs=[row(d), row(d), row(w_pq.shape[1])],
        out_shape=[jax.ShapeDtypeStruct((r, d), F32), jax.ShapeDtypeStruct((r, d), BF16),
                   jax.ShapeDtypeStruct((r, w_pq.shape[1]), BF16)],
        compiler_params=_cparams(("arbitrary",)),
        name="post",
    )(o_gla, o_dsa, x, gt1, g2.reshape(1, d), sc2, sh2, w_out, w_pq)


PEER_NCAND = PEER_TOPK + 1


def _top_rows(work, n):
    vals = []
    for r in range(n):
        m = jnp.max(work, axis=0, keepdims=True)
        vals.append(m)
        if r + 1 < n:
            work = jnp.where(work == m, -jnp.inf, work)
    return vals


def _peer_sel_kernel(pq_ref, k1_ref, k2_ref, e1_ref, e2_ref, s2_ref, c_ref):
    tn = pq_ref.shape[0]
    n = PEER_NCAND
    npad = -(-n // SUBLANE) * SUBLANE
    for h in range(PEER_HEADS):
        pqh = pq_ref[:, h * PEER_DKEY:(h + 1) * PEER_DKEY]
        s1 = _dot_nt(k1_ref[h], pqh)
        s2 = _dot_nt(k2_ref[h], pqh)
        v1 = _top_rows(s1, n)
        v2 = _top_rows(s2, n)
        v2a = jnp.concatenate(v2 + [jnp.full((npad - n, tn), -jnp.inf, F32)], axis=0)
        cands = [v1[0] + v2a] + [v1[a] + v2a[:SUBLANE] for a in range(1, n)]
        assert all(n // (a + 1) <= SUBLANE for a in range(1, n))
        c = _top_rows(jnp.concatenate(cands, axis=0), n)
        thr = 0.5 * (c[PEER_TOPK - 1] + c[PEER_TOPK])
        z = jnp.ones_like(c[0])
        for r in range(1, PEER_TOPK):
            z = z + jnp.exp(c[r] - c[0])
        e1_ref[h] = jnp.exp(s1 - v1[0])
        e2_ref[h] = jnp.exp(s2 - v2[0]) / z
        s2_ref[h] = s2
        c_ref[h] = thr - s1


def _peer_sel(pq, k1p, k2p, tn):
    n, w = pq.shape
    blk = pl.BlockSpec((PEER_HEADS, PEER_NKEYS, tn), lambda i: (0, 0, i))
    full = lambda shp: pl.BlockSpec(shp, lambda i: (0,) * len(shp))
    shp = jax.ShapeDtypeStruct((PEER_HEADS, PEER_NKEYS, n), F32)
    return pl.pallas_call(
        _peer_sel_kernel,
        grid=(n // tn,),
        in_specs=[pl.BlockSpec((tn, w), lambda i: (i, 0)), full(k1p.shape), full(k2p.shape)],
        out_specs=[blk] * 4,
        out_shape=[shp] * 4,
        compiler_params=_cparams(("arbitrary",)),
        name="peer_sel",
    )(pq, k1p, k2p)


PEER_TN = 512
PEER_ROWS = 8


PEER_JC = 16


def _peer_kernel(h2_ref, e1_ref, e2_ref, s2_ref, c_ref, u_ref, vt_ref, x1_ref, gt_ref, gf_ref,
                 y_ref, acc_scr, a_scr, w_scr, *, final):
    j = pl.program_id(1)

    @pl.when(j == 0)
    def _():
        acc_scr[...] = jnp.zeros(acc_scr.shape, F32)

    a_scr[...] = _dot_nt(u_ref[...], h2_ref[...])
    for tb in range(a_scr.shape[1] // LANE):
        ts = slice(tb * LANE, (tb + 1) * LANE)

        def gate_block(jc, carry):
            j0 = pl.multiple_of(jc * PEER_JC, PEER_JC)
            gates = [jnp.zeros((PEER_JC, LANE), F32) for _ in range(PEER_ROWS)]
            for h in range(PEER_HEADS):
                s2 = s2_ref[h, pl.ds(j0, PEER_JC), ts]
                e2 = e2_ref[h, pl.ds(j0, PEER_JC), ts]
                for r in range(PEER_ROWS):
                    gates[r] = gates[r] + e1_ref[h, r:r + 1, ts] * jnp.where(s2 >= c_ref[h, r:r + 1, ts], e2, 0.0)
            for r in range(PEER_ROWS):
                rows = pl.ds(r * PEER_NKEYS + j0, PEER_JC)
                a = a_scr[rows, ts]
                gelu = 0.5 * a * (1.0 + lax.erf(a * (2.0 ** -0.5)))
                w_scr[rows, ts] = (gates[r] * gelu).astype(BF16)
            return carry

        lax.fori_loop(0, PEER_NKEYS // PEER_JC, gate_block, 0)
    acc_scr[...] += _dot(vt_ref[...], w_scr[...])

    @pl.when(j == pl.num_programs(1) - 1)
    def _():
        x2 = x1_ref[...] + gt_ref[...] * acc_scr[...].T
        y_ref[...] = _rms(x2, gf_ref[...]) if final else x2


def _peer(h2, e1, e2, s2, c, u_b, vt_b, x1, gt2, g_final, final):
    n, d = x1.shape
    tn = PEER_TN
    te = PEER_ROWS * PEER_NKEYS
    n_exp = u_b.shape[0]
    mod_spec = (pl.BlockSpec((1, d), lambda i, j: (0, 0)) if gt2.shape[0] == 1
                else pl.BlockSpec((tn, d), lambda i, j: (i, 0)))
    sel = pl.BlockSpec((PEER_HEADS, PEER_NKEYS, tn), lambda i, j: (0, 0, i))
    sel_rows = pl.BlockSpec((PEER_HEADS, PEER_ROWS, tn), lambda i, j: (0, j, i))
    row = pl.BlockSpec((tn, d), lambda i, j: (i, 0))
    return pl.pallas_call(
        functools.partial(_peer_kernel, final=final),
        grid=(n // tn, n_exp // te),
        in_specs=[row, sel_rows, sel, sel, sel_rows,
                  pl.BlockSpec((te, d), lambda i, j: (j, 0)),
                  pl.BlockSpec((d, te), lambda i, j: (0, j)),
                  row, mod_spec, pl.BlockSpec((1, d), lambda i, j: (0, 0))],
        out_specs=row,
        out_shape=jax.ShapeDtypeStruct((n, d), F32),
        scratch_shapes=[pltpu.VMEM((d, tn), F32), pltpu.VMEM((te, tn), F32), pltpu.VMEM((te, tn), BF16)],
        compiler_params=_cparams(("arbitrary", "arbitrary")),
        name="peer",
    )(h2, e1, e2, s2, c, u_b, vt_b, x1, gt2, g_final.reshape(1, d))


def _pad_in_proj(w_in):
    parts, o = [], 0
    for s, ps in zip(SPLIT_SIZES, _PAD_SIZES):
        parts.append(jnp.pad(w_in[:, o:o + s], ((0, 0), (0, ps - s))))
        o += s
    return jnp.concatenate(parts, axis=1).astype(BF16)


def _pad_peer_keys(keys, lo):
    half = keys.shape[-1]
    return jnp.pad(keys, ((0, 0), (0, 0), (lo, PEER_DKEY - half - lo))).astype(BF16)


def _pad_axis(a, axis, size):
    pad = [(0, 0)] * a.ndim
    pad[axis] = (0, size - a.shape[axis])
    return jnp.pad(a, pad)


def _rows_mod(mod, reps):
    parts = jnp.split(mod, 6, axis=-1)
    if mod.shape[0] == 1:
        return parts
    return [jnp.repeat(p, reps, axis=0) for p in parts]


def kernel(x_prompt, x_sample, cache_k, cache_v, cache_kidx, state_gla, page_table, c_prompt, c_sample,
           w_ada, b_ada, g_norm1, g_norm2, w_in, w_gate_a2, b_gate_a2, g_gla_out, g_dsa_out, w_out,
           t5_bias, w_peer_q, peer_keys1, peer_keys2, peer_u, peer_v, g_final):
    depth = w_ada.shape[0]
    B, T, D = x_prompt.shape
    Bd, Ts, _ = x_sample.shape
    xp = x_prompt.reshape(B * T, D)
    xs = x_sample.reshape(Bd * Ts, D)
    R = DSA_S_ROWS
    outs = [[] for _ in range(8)]
    for l in range(depth):
        w_in_p = _pad_in_proj(w_in[l])
        wa2_p = _pad_axis(w_gate_a2[l], 0, LANE).astype(BF16)
        w_out_b = w_out[l].astype(BF16)
        w_pq_b = w_peer_q[l].astype(BF16)
        k1p = _pad_peer_keys(peer_keys1[l], 0)
        k2p = _pad_peer_keys(peer_keys2[l], PEER_DKEY // 2)
        u_b = peer_u[l].astype(BF16)
        vt_b = peer_v[l].T.astype(BF16)
        last = l == depth - 1

        c_all = jnp.concatenate([c_prompt, c_sample], axis=0)
        c_all = _pad_axis(c_all, 0, -(-(B + Bd) // SUBLANE) * SUBLANE)
        mod = _ada(c_all, w_ada[l], b_ada[l])
        mod_p = _rows_mod(mod[:B], T)
        mod_s = _rows_mod(mod[B:B + Bd], Ts)

        sh1, sc1, gt1, sh2, sc2, gt2 = mod_p
        (gq, gk, gv, gg, lg, dq, dk, dv, dkb, dvb, qi, ki, kib, wi) = _proj(
            xp, g_norm1[l], sc1, sh1, w_in_p, wa2_p, b_gate_a2[l], 256)
        s0p = jnp.zeros((B, GLA_HEADS, GLA_DK, GLA_DV), F32)
        r3 = lambda a: a.reshape(B, T, a.shape[-1])
        o_gla, st_p = _gla(r3(gq), r3(gk), r3(gv), r3(gg), r3(lg), g_gla_out[l], s0p, min(GLA_CHUNK, T))
        seq = lambda a, b: a[b * T:(b + 1) * T]
        o_dsa = jnp.concatenate([
            _dsa_prompt(seq(qi, b).T, seq(wi, b)[:, :IDX_HEADS].T, seq(kib, b),
                        seq(dq, b).T, seq(dkb, b), seq(dvb, b).T, t5_bias, g_dsa_out[l])
            for b in range(B)], axis=0)
        x1, h2, pq = _post(o_gla.reshape(B * T, GLA_W), o_dsa, xp, gt1, g_norm2[l], sc2, sh2, w_out_b, w_pq_b, 256)
        e1, e2, s2, cth = _peer_sel(pq, k1p, k2p, 256)
        xp = _peer(h2, e1, e2, s2, cth, u_b, vt_b, x1, gt2, g_final, last)
        outs[0].append(dk.reshape(B, T, DSA_HEADS, DSA_DH))
        outs[1].append(dv.reshape(B, T, DSA_HEADS, DSA_DH))
        outs[2].append(ki.reshape(B, T, IDX_DH))
        outs[3].append(st_p)

        sh1, sc1, gt1, sh2, sc2, gt2 = mod_s
        (gq, gk, gv, gg, lg, dq, dk, dv, dkb, dvb, qi, ki, kib, wi) = _proj(
            xs, g_norm1[l], sc1, sh1, w_in_p, wa2_p, b_gate_a2[l], 256)
        tokpad = lambda a, n: _pad_axis(a.reshape(Bd, Ts, a.shape[-1]), 1, n)
        o_gla, st_s = _gla(tokpad(gq, R), tokpad(gk, R), tokpad(gv, R), tokpad(gg, R), tokpad(lg, R),
                           g_gla_out[l], state_gla[l], R)
        qi_s = _pad_axis(qi.reshape(Bd, Ts, IDX_HEADS, IDX_DH).transpose(0, 2, 1, 3), 2, R)
        wi_s = _pad_axis(wi[:, :IDX_HEADS].reshape(Bd, Ts, IDX_HEADS).transpose(0, 2, 1), 2, R)
        o_dsa = _dsa_sample(page_table, qi_s.reshape(Bd, IDX_HEADS * R, IDX_DH),
                            wi_s.reshape(Bd, IDX_HEADS * R, 1), tokpad(kib, LANE), tokpad(dq, R),
                            tokpad(dkb, LANE), tokpad(dvb, LANE),
                            cache_kidx, cache_k, cache_v, t5_bias, g_dsa_out[l], Ts, l)
        x1, h2, pq = _post(o_gla[:, :Ts].reshape(Bd * Ts, GLA_W), o_dsa[:, :Ts].reshape(Bd * Ts, DSA_W),
                           xs, gt1, g_norm2[l], sc2, sh2, w_out_b, w_pq_b, 256)
        e1, e2, s2, cth = _peer_sel(pq, k1p, k2p, 256)
        xs = _peer(h2, e1, e2, s2, cth, u_b, vt_b, x1, gt2, g_final, last)
        outs[4].append(dk.reshape(Bd, Ts, DSA_HEADS, DSA_DH))
        outs[5].append(dv.reshape(Bd, Ts, DSA_HEADS, DSA_DH))
        outs[6].append(ki.reshape(Bd, Ts, IDX_DH))
        outs[7].append(st_s)

    return (xp.reshape(B, T, D), xs.reshape(Bd, Ts, D)) + tuple(jnp.stack(o) for o in outs)
```

```python
import functools
import math

import numpy as np
import jax
import jax.numpy as jnp
from jax import lax
from jax.experimental import pallas as pl
from jax.experimental.pallas import tpu as pltpu

F32 = jnp.float32
BF16 = jnp.bfloat16
I32 = jnp.int32

EPS = 1e-6
GLA_HEADS, GLA_DK, GLA_DV = 4, 64, 128
GLA_GATE_RANK = 16
GLA_GATE_NORM = 16.0
GLA_QK = GLA_HEADS * GLA_DK
GLA_W = GLA_HEADS * GLA_DV
DSA_HEADS, DSA_DH = 4, 128
DSA_W = DSA_HEADS * DSA_DH
IDX_HEADS, IDX_DH = 8, 64
IDX_W = IDX_HEADS * IDX_DH
DSA_TOPK_MAX = 256
REL_BUCKETS, REL_MAX_DIST = 32, 128
PEER_HEADS, PEER_NKEYS, PEER_DKEY, PEER_TOPK = 8, 128, 128, 16
SPLIT_SIZES = (GLA_QK, GLA_QK, GLA_W, GLA_W, GLA_GATE_RANK,
               DSA_W, DSA_W, DSA_W, IDX_W, IDX_DH, IDX_HEADS)

LANE = 128
SUBLANE = 8
NEG = -1e30
VMEM_LIMIT = 56 * 1024 * 1024

_PAD_SIZES = tuple(-(-s // LANE) * LANE for s in SPLIT_SIZES)
_PAD_OFFS = tuple(int(v) for v in np.cumsum((0,) + _PAD_SIZES))
(O_GQ, O_GK, O_GV, O_GG, O_GA, O_DQ, O_DK, O_DV, O_QI, O_KI, O_WI, D_IN_PAD) = _PAD_OFFS


def _cparams(sem):
    return pltpu.CompilerParams(dimension_semantics=sem, vmem_limit_bytes=VMEM_LIMIT)


def _dot(a, b):
    return jnp.dot(a, b, preferred_element_type=F32)


def _dot_nt(a, b):
    return lax.dot_general(a, b, (((1,), (1,)), ((), ())), preferred_element_type=F32)


def _rms(x, g):
    return x * lax.rsqrt(jnp.mean(x * x, axis=-1, keepdims=True) + EPS) * g


def _ada_kernel(c_ref, w_ref, b_ref, o_ref):
    c = c_ref[...]
    s = c * jax.nn.sigmoid(c)
    o_ref[...] = _dot(s.astype(BF16), w_ref[...].astype(BF16)) + b_ref[...]


def _ada(c, w, b):
    m, d = c.shape
    n = w.shape[1]
    tn = 1024
    return pl.pallas_call(
        _ada_kernel,
        grid=(n // tn,),
        in_specs=[pl.BlockSpec((m, d), lambda j: (0, 0)),
                  pl.BlockSpec((d, tn), lambda j: (0, j)),
                  pl.BlockSpec((1, tn), lambda j: (0, j))],
        out_specs=pl.BlockSpec((m, tn), lambda j: (0, j)),
        out_shape=jax.ShapeDtypeStruct((m, n), F32),
        compiler_params=_cparams(("arbitrary",)),
        name="ada",
    )(c, w, b.reshape(1, n))


def _proj_kernel(x_ref, g_ref, sc_ref, sh_ref, w_ref, wa2_ref, ba2_ref,
                 q_ref, k_ref, v_ref, gg_ref, lg_ref, dq_ref, dk_ref, dv_ref,
                 dkb_ref, dvb_ref, qi_ref, ki_ref, kib_ref, wi_ref):
    x = x_ref[...]
    h = _rms(x, g_ref[...]) * (1.0 + sc_ref[...]) + sh_ref[...]
    p = _dot(h.astype(BF16), w_ref[...])
    q_ref[...] = p[:, O_GQ:O_GK] * (GLA_DK ** -0.5)
    k_ref[...] = p[:, O_GK:O_GV]
    v_ref[...] = p[:, O_GV:O_GG]
    gg_ref[...] = p[:, O_GG:O_GA]
    z = _dot(p[:, O_GA:O_DQ].astype(BF16), wa2_ref[...]) + ba2_ref[...]
    lg_ref[...] = (jnp.minimum(z, 0.0) - jnp.log(1.0 + jnp.exp(-jnp.abs(z)))) * (1.0 / GLA_GATE_NORM)
    dq_ref[...] = (p[:, O_DQ:O_DK] * (DSA_DH ** -0.5)).astype(BF16)
    dk = p[:, O_DK:O_DV]
    dv = p[:, O_DV:O_QI]
    dk_ref[...] = dk
    dv_ref[...] = dv
    dkb_ref[...] = dk.astype(BF16)
    dvb_ref[...] = dv.astype(BF16)
    qi_ref[...] = (p[:, O_QI:O_KI] * (IDX_DH ** -0.5)).astype(BF16)
    ki = p[:, O_KI:O_KI + IDX_DH]
    ki_ref[...] = ki
    kib_ref[...] = ki.astype(BF16)
    wi_ref[...] = p[:, O_WI:D_IN_PAD] * (IDX_HEADS ** -0.5)


def _proj(x, g1, sc, sh, w_in_p, wa2_p, ba2, tm):
    r, d = x.shape
    rows_mod = sc.shape[0]
    mod_spec = (pl.BlockSpec((1, d), lambda i: (0, 0)) if rows_mod == 1
                else pl.BlockSpec((tm, d), lambda i: (i, 0)))
    full = lambda shp: pl.BlockSpec(shp, lambda i: (0,) * len(shp))
    widths = [(GLA_QK, F32), (GLA_QK, F32), (GLA_W, F32), (GLA_W, F32), (GLA_QK, F32),
              (DSA_W, BF16), (DSA_W, F32), (DSA_W, F32), (DSA_W, BF16), (DSA_W, BF16),
              (IDX_W, BF16), (IDX_DH, F32), (IDX_DH, BF16), (LANE, F32)]
    return pl.pallas_call(
        _proj_kernel,
        grid=(r // tm,),
        in_specs=[pl.BlockSpec((tm, d), lambda i: (i, 0)), full((1, d)), mod_spec, mod_spec,
                  full(w_in_p.shape), full(wa2_p.shape), full((1, GLA_QK))],
        out_specs=[pl.BlockSpec((tm, w), lambda i: (i, 0)) for w, _ in widths],
        out_shape=[jax.ShapeDtypeStruct((r, w), dt) for w, dt in widths],
        compiler_params=_cparams(("arbitrary",)),
        name="proj",
    )(x, g1.reshape(1, d), sc, sh, w_in_p, wa2_p, ba2.reshape(1, GLA_QK))


GLA_SUB = 16
GLA_CHUNK = 128


def _gla_kernel(q_ref, k_ref, v_ref, gg_ref, lg_ref, go_ref, s0_ref, o_ref, s_ref, st_scr):
    c = pl.program_id(1)

    @pl.when(c == 0)
    def _():
        st_scr[...] = s0_ref[0]

    q = q_ref[0]
    k = k_ref[0]
    v = v_ref[0]
    lg = lg_ref[0]
    C = q.shape[0]
    rows = lax.broadcasted_iota(I32, (C, C), 0)
    cols = lax.broadcasted_iota(I32, (C, C), 1)
    tri = (rows >= cols).astype(F32)
    b = jnp.dot(tri, lg, preferred_element_type=F32, precision=lax.Precision.HIGHEST)
    b_last = b[C - 1:C, :]
    qc = q * jnp.exp(b)
    kc = k * jnp.exp(b_last - b)
    kcT = kc.T
    dec_col = jnp.broadcast_to(jnp.exp(jnp.sum(lg.T, axis=1, keepdims=True)), (GLA_QK, GLA_DV))
    go = go_ref[...]
    outs = []
    for h in range(GLA_HEADS):
        sl = slice(h * GLA_DK, (h + 1) * GLA_DK)
        vh = v[:, h * GLA_DV:(h + 1) * GLA_DV].astype(BF16)
        S = st_scr[h]
        o = _dot(qc[:, sl].astype(BF16), S.astype(BF16))
        o_rows = []
        sub = min(GLA_SUB, C)
        for blk in range(C // sub):
            r0, r1 = blk * sub, (blk + 1) * sub
            bref = b[r0 - 1:r0, sl] if blk > 0 else jnp.zeros((1, GLA_DK), F32)
            qt = q[r0:r1, sl] * jnp.exp(b[r0:r1, sl] - bref)
            kt = k[:r1, sl] * jnp.exp(bref - b[:r1, sl])
            a = _dot_nt(qt.astype(BF16), kt.astype(BF16))
            causal = (lax.broadcasted_iota(I32, (sub, r1), 1)
                      <= lax.broadcasted_iota(I32, (sub, r1), 0) + r0)
            a = jnp.where(causal, a, 0.0)
            o_rows.append(_dot(a.astype(BF16), vh[:r1]))
        o = o + (jnp.concatenate(o_rows, axis=0) if len(o_rows) > 1 else o_rows[0])
        st_scr[h] = dec_col[sl] * S + _dot(kcT[sl].astype(BF16), vh)
        outs.append(_rms(o, go))
    o_all = jnp.concatenate(outs, axis=1)
    gg = gg_ref[0]
    o_ref[0] = (o_all * (gg * jax.nn.sigmoid(gg))).astype(BF16)

    @pl.when(c == pl.num_programs(1) - 1)
    def _():
        s_ref[0] = st_scr[...]


def _gla(q, k, v, gg, lg, g_out, s0, chunk):
    B, T, _ = q.shape
    n = T // chunk
    tok = lambda w: pl.BlockSpec((1, chunk, w), lambda b, c: (b, c, 0))
    st = pl.BlockSpec((1, GLA_HEADS, GLA_DK, GLA_DV), lambda b, c: (b, 0, 0, 0))
    return pl.pallas_call(
        _gla_kernel,
        grid=(B, n),
        in_specs=[tok(GLA_QK), tok(GLA_QK), tok(GLA_W), tok(GLA_W), tok(GLA_QK),
                  pl.BlockSpec((1, GLA_DV), lambda b, c: (0, 0)), st],
        out_specs=[tok(GLA_W), st],
        out_shape=[jax.ShapeDtypeStruct((B, T, GLA_W), BF16),
                   jax.ShapeDtypeStruct((B, GLA_HEADS, GLA_DK, GLA_DV), F32)],
        scratch_shapes=[pltpu.VMEM((GLA_HEADS, GLA_DK, GLA_DV), F32)],
        compiler_params=_cparams(("arbitrary", "arbitrary")),
        name="gla",
    )(q, k, v, gg, lg, g_out.reshape(1, GLA_DV), s0)


def _t5_bucket_np(d):
    max_exact = REL_BUCKETS // 2
    d = np.maximum(d, 0)
    large = max_exact + (np.log(np.maximum(d, 1).astype(np.float32) / np.float32(max_exact))
                         / np.float32(math.log(REL_MAX_DIST / max_exact))
                         * np.float32(REL_BUCKETS - max_exact)).astype(np.int32)
    large = np.minimum(large, REL_BUCKETS - 1)
    return np.where(d < max_exact, d, large).astype(np.int32)


T5_FAR = 113
assert int(_t5_bucket_np(np.arange(T5_FAR, 1 << 16)).min()) == REL_BUCKETS - 1


def _t5_kernel(bias_ref, idx_ref, o_ref):
    idx = idx_ref[0]
    for h in range(DSA_HEADS):
        acc = jnp.zeros(idx.shape, F32)
        for bkt in range(REL_BUCKETS):
            acc = jnp.where(idx == bkt, bias_ref[bkt, h], acc)
        o_ref[0, h] = acc - bias_ref[REL_BUCKETS - 1, h]


def _t5_tiles(t5_bias, dist_tiles):
    idx = jnp.asarray(_t5_bucket_np(dist_tiles))
    n, r, c = idx.shape
    return pl.pallas_call(
        _t5_kernel,
        grid=(n,),
        in_specs=[pl.BlockSpec(memory_space=pltpu.SMEM),
                  pl.BlockSpec((1, r, c), lambda i: (i, 0, 0))],
        out_specs=pl.BlockSpec((1, DSA_HEADS, r, c), lambda i: (i, 0, 0, 0)),
        out_shape=jax.ShapeDtypeStruct((n, DSA_HEADS, r, c), F32),
        compiler_params=_cparams(("arbitrary",)),
        name="t5tiles",
    )(t5_bias, idx)


def _key_to_f32(m):
    bits = m ^ ((m >> 31) & jnp.int32(0x7FFFFFFF))
    return pltpu.bitcast(bits, F32)


KEY_NEG_INF = int(np.int32(-2139095041))
KEY_POS_INF = 0x7F800000


def _avg_floor(lo, hi):
    return (lo >> 1) + (hi >> 1) + (lo & hi & 1)


DSA_QB = 256
DSA_KV_STEP = 2048


def _dsa_prompt_kernel(qiT_ref, wiT_ref, ki_ref, qT_ref, k_ref, vT_ref, tb_ref, g_ref,
                       o_ref, sc_scr, thr_scr, m_scr, l_scr, acc_scr, *, topk, n_sub):
    i = pl.program_id(0)
    j = pl.program_id(1)
    QB = DSA_QB
    NV = QB // SUBLANE
    ntile = i + 1
    shape3 = (NV, SUBLANE, QB)
    key3 = (lax.broadcasted_iota(I32, shape3, 0) * SUBLANE + lax.broadcasted_iota(I32, shape3, 1))
    qry3 = lax.broadcasted_iota(I32, shape3, 2)

    def tile3(t):
        c0 = pl.multiple_of(t * QB, QB)
        return sc_scr[pl.ds(c0, QB), :].reshape(shape3)

    def count(pred3_of_tile):
        def body(t, cnt8):
            return cnt8 + jnp.sum(jnp.where(pred3_of_tile(t), 1.0, 0.0), axis=0)
        cnt8 = lax.fori_loop(0, ntile, body, jnp.zeros((SUBLANE, QB), F32))
        return jnp.sum(cnt8, axis=0, keepdims=True)

    @pl.when(j == 0)
    def _scores_and_threshold():
        wiT = wiT_ref[...]

        def score_tile(t, carry):
            c0 = pl.multiple_of(t * QB, QB)
            kt = ki_ref[pl.ds(c0, QB), :]
            acc = jnp.zeros((QB, QB), F32)
            for h in range(IDX_HEADS):
                s = _dot(kt, qiT_ref[h * IDX_DH:(h + 1) * IDX_DH, :])
                acc = acc + jnp.maximum(s, 0.0) * wiT[h:h + 1, :]
            acc3 = jnp.where((t < i) | (key3 <= qry3), acc.reshape(shape3), -jnp.inf)
            sc_scr[pl.ds(c0, QB), :] = acc3.reshape(QB, QB)
            return carry

        lax.fori_loop(0, ntile, score_tile, 0)

        def bis_cond(c):
            it, n_open, _, _ = c
            return (it < 32) & (n_open > 0)

        def bis_body(c):
            it, _, lo, hi = c
            mid = _avg_floor(lo, hi)
            midf = _key_to_f32(mid)
            cnt = count(lambda t: tile3(t) > midf[None])
            is_open = mid > lo
            ok = cnt <= float(topk - 1)
            exact = cnt == float(topk)
            lo2 = jnp.where(is_open & ~ok, mid, lo)
            hi2 = jnp.where(is_open & ok, mid, jnp.where(is_open & exact, mid + 1, hi))
            n_open = jnp.max(jnp.where(_avg_floor(lo2, hi2) > lo2, 1, 0))
            return it + 1, n_open, lo2, hi2

        lo0 = jnp.full((SUBLANE, QB), KEY_NEG_INF - 1, I32)
        hi0 = jnp.full((SUBLANE, QB), KEY_POS_INF, I32)
        _, _, _, hi = lax.while_loop(bis_cond, bis_body, (jnp.int32(0), jnp.int32(1), lo0, hi0))
        thr8 = _key_to_f32(hi)
        n_ge = count(lambda t: tile3(t) >= thr8[None])

        def short_of(thr, n):
            return jnp.max(jnp.where((n < float(topk)) & (thr[:1] > -jnp.inf), 1, 0))

        def lower_cond(c):
            it, n_short, _, _ = c
            return (it < 8) & (n_short > 0)

        def lower_body(c):
            it, _, thr, n = c

            def below(t, b8):
                s3 = tile3(t)
                return jnp.maximum(b8, jnp.max(jnp.where(s3 < thr[None], s3, -jnp.inf), axis=0))

            b8 = lax.fori_loop(0, ntile, below, jnp.full((SUBLANE, QB), -jnp.inf, F32))
            nxt = jnp.max(b8, axis=0, keepdims=True)
            thr2 = jnp.where((n < float(topk)) & (thr[:1] > -jnp.inf), jnp.broadcast_to(nxt, thr.shape), thr)
            n2 = count(lambda t: tile3(t) >= thr2[None])
            return it + 1, short_of(thr2, n2), thr2, n2

        _, _, thr8, n_ge = lax.while_loop(lower_cond, lower_body, (jnp.int32(0), short_of(thr8, n_ge), thr8, n_ge))
        thr_scr[...] = thr8

        excess = jnp.where((thr8[:1] > -jnp.inf) & (n_ge > float(topk)), 1.0, 0.0)

        @pl.when(jnp.max(excess) > 0.0)
        def _ties():
            need = float(topk) - count(lambda t: tile3(t) > thr8[None])

            def pbis(_, lohi):
                lo, hi = lohi
                mid = (lo + hi) >> 1
                n = count(lambda t: (tile3(t) == thr8[None]) & (key3 + t * QB <= mid[None]))
                ok = n >= need
                return jnp.where(ok, lo, mid), jnp.where(ok, mid, hi)

            n_keys = sc_scr.shape[0]
            nbits = int(math.ceil(math.log2(n_keys))) + 1
            plo = jnp.full((SUBLANE, QB), -1, I32)
            phi = jnp.full((SUBLANE, QB), n_keys - 1, I32)
            _, pcut = lax.fori_loop(0, nbits, pbis, (plo, phi))
            pcut = jnp.where(excess > 0.0, pcut, n_keys)

            def drop(t, carry):
                c0 = pl.multiple_of(t * QB, QB)
                s3 = tile3(t)
                s3 = jnp.where((s3 == thr8[None]) & (key3 + t * QB > pcut[None]), -jnp.inf, s3)
                sc_scr[pl.ds(c0, QB), :] = s3.reshape(QB, QB)
                return carry

            lax.fori_loop(0, ntile, drop, 0)

        def around(t, c):
            a8, b8 = c
            s3 = tile3(t)
            ge = s3 >= thr8[None]
            a8 = jnp.minimum(a8, jnp.min(jnp.where(ge, s3, jnp.inf), axis=0))
            b8 = jnp.maximum(b8, jnp.max(jnp.where(ge, -jnp.inf, s3), axis=0))
            return a8, b8

        a8, b8 = lax.fori_loop(0, ntile, around, (jnp.full((SUBLANE, QB), jnp.inf, F32),
                                                  jnp.full((SUBLANE, QB), -jnp.inf, F32)))
        a = jnp.min(a8, axis=0, keepdims=True)
        bmax = jnp.max(b8, axis=0, keepdims=True)
        mid = 0.5 * (a + bmax)
        centred = jnp.where(bmax > -jnp.inf, jnp.where(mid > bmax, mid, a), thr8[:1])
        thr_scr[...] = jnp.broadcast_to(centred, (SUBLANE, QB))

        m_scr[...] = jnp.full(m_scr.shape, NEG, F32)
        l_scr[...] = jnp.zeros(l_scr.shape, F32)
        acc_scr[...] = jnp.zeros(acc_scr.shape, F32)

    thr8 = thr_scr[...]

    def attend(subs, kind):
        r0s = [pl.multiple_of(s * QB, QB) for s in subs]
        sels = [tile3(j * n_sub + s) >= thr8[None] for s in subs]
        if kind == 2:
            sels = [sel & (key3 <= qry3) for sel in sels]
        for h in range(DSA_HEADS):
            hs = slice(h * DSA_DH, (h + 1) * DSA_DH)
            l3s = []
            for r0, sel in zip(r0s, sels):
                logit = _dot(k_ref[pl.ds(r0, QB), hs], qT_ref[hs, :])
                if kind != 0:
                    logit = logit + tb_ref[kind - 1, h]
                l3s.append(jnp.where(sel, logit.reshape(shape3), NEG))
            m_old = m_scr[h]
            cmax = functools.reduce(jnp.maximum, [jnp.max(l3, axis=0) for l3 in l3s])
            m_new = jnp.maximum(m_old, jnp.max(cmax, axis=0, keepdims=True))
            alpha = jnp.exp(m_old - m_new)
            p3s = [jnp.exp(l3 - m_new[None]) for l3 in l3s]
            l_scr[h] = alpha * l_scr[h] + sum(jnp.sum(p3, axis=0) for p3 in p3s)
            pv = sum(_dot(vT_ref[hs, pl.ds(r0, QB)], p3.reshape(QB, QB).astype(BF16))
                     for r0, p3 in zip(r0s, p3s))
            acc3 = acc_scr[hs, :].reshape(DSA_DH // SUBLANE, SUBLANE, QB) * alpha[None]
            acc_scr[hs, :] = acc3.reshape(DSA_DH, QB) + pv
            m_scr[h] = m_new

    t0 = j * n_sub
    n_far = jnp.clip(i - 1 - t0, 0, n_sub)

    def far_quad(s, carry):
        attend([4 * s, 4 * s + 1, 4 * s + 2, 4 * s + 3], 0)
        return carry

    @pl.when(n_far == 8)
    def _far_all():
        attend(list(range(8)), 0)

    lax.fori_loop(0, jnp.where(n_far == 8, 0, n_far // 4), far_quad, 0)
    n_done = (n_far // 4) * 4

    @pl.when(n_far - n_done >= 2)
    def _far_pair():
        attend([n_done, n_done + 1], 0)

    @pl.when(n_far % 2 == 1)
    def _far_odd():
        attend([n_far - 1], 0)

    @pl.when((i - 1 >= t0) & (i - 1 < t0 + n_sub))
    def _prev():
        attend([i - 1 - t0], 1)

    @pl.when((i >= t0) & (i < t0 + n_sub))
    def _diag():
        attend([i - t0], 2)

    @pl.when(j == pl.num_programs(1) - 1)
    def _finish():
        outs = []
        for h in range(DSA_HEADS):
            hs = slice(h * DSA_DH, (h + 1) * DSA_DH)
            l_tot = jnp.sum(l_scr[h], axis=0, keepdims=True)
            outs.append(acc_scr[hs, :] / l_tot)
        oT = jnp.concatenate(outs, axis=0)
        inv = lax.rsqrt(jnp.mean(oT * oT, axis=0, keepdims=True) + EPS)
        gcol = jnp.concatenate([g_ref[...]] * (QB // LANE), axis=1)
        o_ref[...] = (oT * inv * gcol).T.astype(BF16)


def _dsa_prompt(qiT, wiT, ki, qT, k, vT, t5_bias, g_dsa):
    T = k.shape[0]
    QB = DSA_QB
    topk = min(DSA_TOPK_MAX, T // 4)
    kv_step = min(DSA_KV_STEP, T)
    assert T % QB == 0 and T % kv_step == 0 and kv_step % QB == 0
    nq, nkv = T // QB, T // kv_step
    kk = np.arange(QB)[:, None]
    qq = np.arange(QB)[None, :]
    tb = _t5_tiles(t5_bias, np.stack([QB + qq - kk, qq - kk]))
    last_kv = lambda i: ((i + 1) * QB - 1) // kv_step
    kv_idx = lambda i, j: jnp.minimum(j, last_kv(i))
    qcol = lambda r: pl.BlockSpec((r, QB), lambda i, j: (0, i))
    g_rep = jnp.broadcast_to(g_dsa.reshape(DSA_W, 1), (DSA_W, LANE))
    return pl.pallas_call(
        functools.partial(_dsa_prompt_kernel, topk=topk, n_sub=kv_step // QB),
        grid=(nq, nkv),
        in_specs=[qcol(IDX_W), qcol(IDX_HEADS), pl.BlockSpec((T, IDX_DH), lambda i, j: (0, 0)),
                  qcol(DSA_W),
                  pl.BlockSpec((kv_step, DSA_W), lambda i, j: (kv_idx(i, j), 0)),
                  pl.BlockSpec((DSA_W, kv_step), lambda i, j: (0, kv_idx(i, j))),
                  pl.BlockSpec((2, DSA_HEADS, QB, QB), lambda i, j: (0, 0, 0, 0)),
                  pl.BlockSpec((DSA_W, LANE), lambda i, j: (0, 0))],
        out_specs=pl.BlockSpec((QB, DSA_W), lambda i, j: (i, 0)),
        out_shape=jax.ShapeDtypeStruct((T, DSA_W), BF16),
        scratch_shapes=[pltpu.VMEM((T, QB), F32), pltpu.VMEM((SUBLANE, QB), F32),
                        pltpu.VMEM((DSA_HEADS, SUBLANE, QB), F32), pltpu.VMEM((DSA_HEADS, SUBLANE, QB), F32),
                        pltpu.VMEM((DSA_W, QB), F32)],
        compiler_params=_cparams(("arbitrary", "arbitrary")),
        name="dsa_p",
    )(qiT, wiT, ki, qT, k, vT, tb, g_rep)


DSA_S_ROWS = 8
DSA_S_GROUP = 32


def _dsa_sample_kernel(pt_ref, qi_ref, wi_ref, kin_ref, q_ref, kn_ref, vn_ref, tb_ref, g_ref,
                       ckidx_hbm, ck_hbm, cv_hbm, o_ref,
                       kidx_buf, sc_scr, kbuf, vbuf, m_scr, l_scr, acc_scr, sem_i, sem_k, sem_v,
                       *, topk, n_new, layer):
    b = pl.program_id(0)
    nb = pl.num_programs(0)
    n_pages = pt_ref.shape[1]
    PS = kidx_buf.shape[2] // n_pages
    past = n_pages * PS
    R = DSA_S_ROWS
    G = DSA_S_GROUP
    GK = G * PS
    n_groups = n_pages // G
    L = past + LANE
    PR = PS * DSA_HEADS
    islot = b % 2

    def kidx_copy(bb, p):
        return pltpu.make_async_copy(ckidx_hbm.at[layer, pt_ref[bb, p]],
                                     kidx_buf.at[bb % 2, :, pl.ds(p * PS, PS)], sem_i.at[bb % 2])

    def kv_copies(bb, g, slot, p):
        page = pt_ref[bb, g * G + p]
        return (pltpu.make_async_copy(ck_hbm.at[layer, page], kbuf.at[slot, pl.ds(p * PR, PR)], sem_k.at[slot]),
                pltpu.make_async_copy(cv_hbm.at[layer, page], vbuf.at[slot, pl.ds(p * PR, PR)], sem_v.at[slot]))

    def start_group(bb, g, slot):
        def body(p, c):
            ck, cv = kv_copies(bb, g, slot, p)
            ck.start()
            cv.start()
            return c
        lax.fori_loop(0, G, body, 0)

    def wait_group(bb, g, slot):
        def body(p, c):
            ck, cv = kv_copies(bb, g, slot, p)
            ck.wait()
            cv.wait()
            return c
        lax.fori_loop(0, G, body, 0)

    def start_kidx(bb):
        def body(p, c):
            kidx_copy(bb, p).start()
            return c
        lax.fori_loop(0, n_pages, body, 0)

    def wait_kidx(bb):
        def body(p, c):
            kidx_copy(bb, p).wait()
            return c
        lax.fori_loop(0, n_pages, body, 0)

    @pl.when(b == 0)
    def _():
        start_kidx(b)
        start_group(b, 0, 0)

    wait_kidx(b)

    @pl.when(b + 1 < nb)
    def _():
        start_kidx(b + 1)

    qi = qi_ref[0]
    wi = wi_ref[0]

    def head_sum(s):
        r = jnp.maximum(s, 0.0) * wi
        tot = r[0:R]
        for h in range(1, IDX_HEADS):
            tot = tot + r[h * R:(h + 1) * R]
        return tot

    def score_chunk(cidx, c):
        c0 = pl.multiple_of(cidx * GK, GK)
        kc = kidx_buf[islot, :, pl.ds(c0, GK)].astype(BF16)
        sc_scr[:, pl.ds(c0, GK)] = head_sum(_dot(qi, kc))
        return c

    lax.fori_loop(0, n_groups, score_chunk, 0)
    trow = lax.broadcasted_iota(I32, (R, LANE), 0)
    tcol = lax.broadcasted_iota(I32, (R, LANE), 1)
    new_ok = (tcol <= trow) & (tcol < n_new)
    sc_scr[:, past:L] = jnp.where(new_ok, head_sum(_dot_nt(qi, kin_ref[0])), -jnp.inf)

    def count(pred):
        return jnp.sum(jnp.where(pred, 1.0, 0.0), axis=1, keepdims=True)

    def bis_cond(c):
        it, n_open, _, _ = c
        return (it < 32) & (n_open > 0)

    def bis_body(c):
        it, _, lo, hi = c
        mid = _avg_floor(lo, hi)
        cnt = count(sc_scr[...] > _key_to_f32(mid)[:, :1])
        is_open = mid > lo
        ok = cnt <= float(topk - 1)
        exact = cnt == float(topk)
        lo2 = jnp.where(is_open & ~ok, mid, lo)
        hi2 = jnp.where(is_open & ok, mid, jnp.where(is_open & exact, mid + 1, hi))
        n_open = jnp.max(jnp.where(_avg_floor(lo2, hi2) > lo2, 1, 0))
        return it + 1, n_open, lo2, hi2

    lo0 = jnp.full((R, LANE), KEY_NEG_INF - 1, I32)
    hi0 = jnp.full((R, LANE), KEY_POS_INF, I32)
    _, _, _, hi = lax.while_loop(bis_cond, bis_body, (jnp.int32(0), jnp.int32(1), lo0, hi0))
    thr = _key_to_f32(hi)[:, :1]
    n_ge = count(sc_scr[...] >= thr)

    def short_of(t, n):
        return jnp.max(jnp.where((n < float(topk)) & (t > -jnp.inf), 1, 0))

    def lower_cond(c):
        it, n_short, _, _ = c
        return (it < 8) & (n_short > 0)

    def lower_body(c):
        it, _, t, n = c
        s = sc_scr[...]
        nxt = jnp.max(jnp.where(s < t, s, -jnp.inf), axis=1, keepdims=True)
        t2 = jnp.where((n < float(topk)) & (t > -jnp.inf), nxt, t)
        n2 = count(sc_scr[...] >= t2)
        return it + 1, short_of(t2, n2), t2, n2

    _, _, thr, n_ge = lax.while_loop(lower_cond, lower_body, (jnp.int32(0), short_of(thr, n_ge), thr, n_ge))
    excess = jnp.where((thr > -jnp.inf) & (n_ge > float(topk)), 1.0, 0.0)

    @pl.when(jnp.max(excess) > 0.0)
    def _ties():
        need = float(topk) - count(sc_scr[...] > thr)
        pos = lax.broadcasted_iota(I32, (R, L), 1)

        def pbis(_, lohi):
            lo, hi = lohi
            mid = (lo + hi) >> 1
            ok = count((sc_scr[...] == thr) & (pos <= mid)) >= need
            return jnp.where(ok, lo, mid), jnp.where(ok, mid, hi)

        nbits = int(math.ceil(math.log2(L))) + 1
        _, pcut = lax.fori_loop(0, nbits, pbis, (jnp.full((R, 1), -1, I32), jnp.full((R, 1), L - 1, I32)))
        pcut = jnp.where(excess > 0.0, pcut, L)
        s = sc_scr[...]
        sc_scr[...] = jnp.where((s == thr) & (pos > pcut), -jnp.inf, s)

    s_all = sc_scr[...]
    ge = s_all >= thr
    a = jnp.min(jnp.where(ge, s_all, jnp.inf), axis=1, keepdims=True)
    bmax = jnp.max(jnp.where(ge, -jnp.inf, s_all), axis=1, keepdims=True)
    mid = 0.5 * (a + bmax)
    thr = jnp.where(bmax > -jnp.inf, jnp.where(mid > bmax, mid, a), thr)

    m_scr[...] = jnp.full(m_scr.shape, NEG, F32)
    l_scr[...] = jnp.zeros(l_scr.shape, F32)
    acc_scr[...] = jnp.zeros(acc_scr.shape, F32)
    q = q_ref[0]

    def softmax_step(h, logit, sel, vh):
        hs = slice(h * DSA_DH, (h + 1) * DSA_DH)
        logit = jnp.where(sel, logit, NEG)
        m_old = m_scr[h]
        m_new = jnp.maximum(m_old, jnp.max(logit, axis=1, keepdims=True))
        alpha = jnp.exp(m_old - m_new)
        p = jnp.exp(logit - m_new[:, :1])
        l_scr[h] = alpha * l_scr[h] + jnp.sum(p, axis=1, keepdims=True)
        acc_scr[:, hs] = alpha * acc_scr[:, hs] + _dot(p.astype(BF16), vh)
        m_scr[h] = m_new

    def attend_group(g, slot, last):
        c0 = pl.multiple_of(g * GK, GK)
        sel = sc_scr[:, pl.ds(c0, GK)] >= thr
        for h in range(DSA_HEADS):
            hs = slice(h * DSA_DH, (h + 1) * DSA_DH)
            logit = _dot_nt(q[:, hs], kbuf[slot, pl.ds(h, GK, stride=DSA_HEADS), :].astype(BF16))
            if last:
                logit = logit + jnp.concatenate([jnp.zeros((R, GK - PS), F32), tb_ref[0, h]], axis=1)
            softmax_step(h, logit, sel, vbuf[slot, pl.ds(h, GK, stride=DSA_HEADS), :].astype(BF16))

    def group_body(g, c):
        slot = g % 2
        wait_group(b, g, slot)

        @pl.when(g + 1 < n_groups)
        def _():
            start_group(b, g + 1, 1 - slot)

        @pl.when((g + 1 == n_groups) & (b + 1 < nb))
        def _():
            start_group(b + 1, 0, 0)

        @pl.when(g < n_groups - 1)
        def _():
            attend_group(g, slot, False)

        @pl.when(g == n_groups - 1)
        def _():
            attend_group(g, slot, True)
        return c

    lax.fori_loop(0, n_groups, group_body, 0)

    sel_new = (sc_scr[:, past:L] >= thr) & new_ok
    kn = kn_ref[0]
    vn = vn_ref[0]
    for h in range(DSA_HEADS):
        hs = slice(h * DSA_DH, (h + 1) * DSA_DH)
        softmax_step(h, _dot_nt(q[:, hs], kn[:, hs]) + tb_ref[1, h], sel_new, vn[:, hs])

    outs = []
    for h in range(DSA_HEADS):
        hs = slice(h * DSA_DH, (h + 1) * DSA_DH)
        outs.append(acc_scr[:, hs] / l_scr[h])
    o_ref[0] = _rms(jnp.concatenate(outs, axis=1), g_ref[...]).astype(BF16)


def _dsa_sample(page_table, qi, wi, ki_new, q, k_new, v_new, cache_kidx, cache_k, cache_v, t5_bias, g_dsa,
                n_new, layer):
    Bd, n_pages = page_table.shape
    depth, n_phys, PS = cache_kidx.shape[:3]
    ckidx_t = jnp.swapaxes(cache_kidx, 2, 3)
    ck2 = cache_k.reshape(depth, n_phys, PS * DSA_HEADS, DSA_DH)
    cv2 = cache_v.reshape(depth, n_phys, PS * DSA_HEADS, DSA_DH)
    past = n_pages * PS
    R = DSA_S_ROWS
    topk = min(DSA_TOPK_MAX, (past + n_new) // 4)
    assert n_pages % (2 * DSA_S_GROUP) == 0 and PS == LANE and n_new <= R
    t = np.arange(R)[:, None]
    c = np.arange(LANE)[None, :]
    tb = _t5_tiles(t5_bias, np.stack([PS + t - c, t - c]))
    GK = DSA_S_GROUP * PS
    per_b = lambda shp: pl.BlockSpec((1,) + shp, lambda b, pt: (b,) + (0,) * len(shp))
    grid_spec = pltpu.PrefetchScalarGridSpec(
        num_scalar_prefetch=1,
        grid=(Bd,),
        in_specs=[per_b((IDX_HEADS * R, IDX_DH)), per_b((IDX_HEADS * R, 1)), per_b((LANE, IDX_DH)),
                  per_b((R, DSA_W)), per_b((LANE, DSA_W)), per_b((LANE, DSA_W)),
                  pl.BlockSpec((2, DSA_HEADS, R, LANE), lambda b, pt: (0, 0, 0, 0)),
                  pl.BlockSpec((1, DSA_W), lambda b, pt: (0, 0)),
                  pl.BlockSpec(memory_space=pl.ANY), pl.BlockSpec(memory_space=pl.ANY),
                  pl.BlockSpec(memory_space=pl.ANY)],
        out_specs=per_b((R, DSA_W)),
        scratch_shapes=[pltpu.VMEM((2, IDX_DH, past), F32), pltpu.VMEM((R, past + LANE), F32),
                        pltpu.VMEM((2, GK * DSA_HEADS, DSA_DH), F32), pltpu.VMEM((2, GK * DSA_HEADS, DSA_DH), F32),
                        pltpu.VMEM((DSA_HEADS, R, LANE), F32), pltpu.VMEM((DSA_HEADS, R, LANE), F32),
                        pltpu.VMEM((R, DSA_W), F32),
                        pltpu.SemaphoreType.DMA((2,)), pltpu.SemaphoreType.DMA((2,)),
                        pltpu.SemaphoreType.DMA((2,))],
    )
    return pl.pallas_call(
        functools.partial(_dsa_sample_kernel, topk=topk, n_new=n_new, layer=layer),
        grid_spec=grid_spec,
        out_shape=jax.ShapeDtypeStruct((Bd, R, DSA_W), BF16),
        compiler_params=_cparams(("arbitrary",)),
        name="dsa_s",
    )(page_table, qi, wi, ki_new, q, k_new, v_new, tb, g_dsa.reshape(1, DSA_W), ckidx_t, ck2, cv2)


def _post_kernel(og_ref, od_ref, x_ref, gt_ref, g2_ref, sc_ref, sh_ref, wo_ref, wpq_ref,
                 x1_ref, h2_ref, pq_ref):
    half = og_ref.shape[1]
    y = _dot(og_ref[...], wo_ref[:half, :]) + _dot(od_ref[...], wo_ref[half:, :])
    x1 = x_ref[...] + gt_ref[...] * y
    x1_ref[...] = x1
    h2 = (_rms(x1, g2_ref[...]) * (1.0 + sc_ref[...]) + sh_ref[...]).astype(BF16)
    h2_ref[...] = h2
    pq_ref[...] = _dot(h2, wpq_ref[...]).astype(BF16)


def _post(o_gla, o_dsa, x, gt1, g2, sc2, sh2, w_out, w_pq, tm):
    r, d = x.shape
    mod_spec = (pl.BlockSpec((1, d), lambda i: (0, 0)) if gt1.shape[0] == 1
                else pl.BlockSpec((tm, d), lambda i: (i, 0)))
    full = lambda shp: pl.BlockSpec(shp, lambda i: (0,) * len(shp))
    row = lambda w: pl.BlockSpec((tm, w), lambda i: (i, 0))
    return pl.pallas_call(
        _post_kernel,
        grid=(r // tm,),
        in_specs=[row(o_gla.shape[1]), row(o_dsa.shape[1]), row(d), mod_spec, full((1, d)), mod_spec, mod_spec,
                  full(w_out.shape), full(w_pq.shape)],
        out_specs=[row(d), row(d), row(w_pq.shape[1])],
        out_shape=[jax.ShapeDtypeStruct((r, d), F32), jax.ShapeDtypeStruct((r, d), BF16),
                   jax.ShapeDtypeStruct((r, w_pq.shape[1]), BF16)],
        compiler_params=_cparams(("arbitrary",)),
        name="post",
    )(o_gla, o_dsa, x, gt1, g2.reshape(1, d), sc2, sh2, w_out, w_pq)


PEER_NCAND = PEER_TOPK + 1


def _top_rows(work, n):
    vals = []
    for r in range(n):
        m = jnp.max(work, axis=0, keepdims=True)
        vals.append(m)
        if r + 1 < n:
            work = jnp.where(work == m, -jnp.inf, work)
    return vals


def _peer_sel_kernel(pq_ref, k1_ref, k2_ref, e1_ref, e2_ref, s2_ref, c_ref):
    tn = pq_ref.shape[0]
    n = PEER_NCAND
    npad = -(-n // SUBLANE) * SUBLANE
    for h in range(PEER_HEADS):
        pqh = pq_ref[:, h * PEER_DKEY:(h + 1) * PEER_DKEY]
        s1 = _dot_nt(k1_ref[h], pqh)
        s2 = _dot_nt(k2_ref[h], pqh)
        v1 = _top_rows(s1, n)
        v2 = _top_rows(s2, n)
        v2a = jnp.concatenate(v2 + [jnp.full((npad - n, tn), -jnp.inf, F32)], axis=0)
        cands = [v1[0] + v2a] + [v1[a] + v2a[:SUBLANE] for a in range(1, n)]
        assert all(n // (a + 1) <= SUBLANE for a in range(1, n))
        c = _top_rows(jnp.concatenate(cands, axis=0), n)
        thr = 0.5 * (c[PEER_TOPK - 1] + c[PEER_TOPK])
        z = jnp.ones_like(c[0])
        for r in range(1, PEER_TOPK):
            z = z + jnp.exp(c[r] - c[0])
        e1_ref[h] = jnp.exp(s1 - v1[0])
        e2_ref[h] = jnp.exp(s2 - v2[0]) / z
        s2_ref[h] = s2
        c_ref[h] = thr - s1


def _peer_sel(pq, k1p, k2p, tn):
    n, w = pq.shape
    blk = pl.BlockSpec((PEER_HEADS, PEER_NKEYS, tn), lambda i: (0, 0, i))
    full = lambda shp: pl.BlockSpec(shp, lambda i: (0,) * len(shp))
    shp = jax.ShapeDtypeStruct((PEER_HEADS, PEER_NKEYS, n), F32)
    return pl.pallas_call(
        _peer_sel_kernel,
        grid=(n // tn,),
        in_specs=[pl.BlockSpec((tn, w), lambda i: (i, 0)), full(k1p.shape), full(k2p.shape)],
        out_specs=[blk] * 4,
        out_shape=[shp] * 4,
        compiler_params=_cparams(("arbitrary",)),
        name="peer_sel",
    )(pq, k1p, k2p)


PEER_TN = 512
PEER_ROWS = 8


PEER_JC = 16


def _peer_kernel(h2_ref, e1_ref, e2_ref, s2_ref, c_ref, u_ref, vt_ref, x1_ref, gt_ref, gf_ref,
                 y_ref, acc_scr, a_scr, w_scr, *, final):
    j = pl.program_id(1)

    @pl.when(j == 0)
    def _():
        acc_scr[...] = jnp.zeros(acc_scr.shape, F32)

    a_scr[...] = _dot_nt(u_ref[...], h2_ref[...])
    for tb in range(a_scr.shape[1] // LANE):
        ts = slice(tb * LANE, (tb + 1) * LANE)

        def gate_block(jc, carry):
            j0 = pl.multiple_of(jc * PEER_JC, PEER_JC)
            gates = [jnp.zeros((PEER_JC, LANE), F32) for _ in range(PEER_ROWS)]
            for h in range(PEER_HEADS):
                s2 = s2_ref[h, pl.ds(j0, PEER_JC), ts]
                e2 = e2_ref[h, pl.ds(j0, PEER_JC), ts]
                for r in range(PEER_ROWS):
                    gates[r] = gates[r] + e1_ref[h, r:r + 1, ts] * jnp.where(s2 >= c_ref[h, r:r + 1, ts], e2, 0.0)
            for r in range(PEER_ROWS):
                rows = pl.ds(r * PEER_NKEYS + j0, PEER_JC)
                a = a_scr[rows, ts]
                gelu = 0.5 * a * (1.0 + lax.erf(a * (2.0 ** -0.5)))
                w_scr[rows, ts] = (gates[r] * gelu).astype(BF16)
            return carry

        lax.fori_loop(0, PEER_NKEYS // PEER_JC, gate_block, 0)
    acc_scr[...] += _dot(vt_ref[...], w_scr[...])

    @pl.when(j == pl.num_programs(1) - 1)
    def _():
        x2 = x1_ref[...] + gt_ref[...] * acc_scr[...].T
        y_ref[...] = _rms(x2, gf_ref[...]) if final else x2


def _peer(h2, e1, e2, s2, c, u_b, vt_b, x1, gt2, g_final, final):
    n, d = x1.shape
    tn = PEER_TN
    te = PEER_ROWS * PEER_NKEYS
    n_exp = u_b.shape[0]
    mod_spec = (pl.BlockSpec((1, d), lambda i, j: (0, 0)) if gt2.shape[0] == 1
                else pl.BlockSpec((tn, d), lambda i, j: (i, 0)))
    sel = pl.BlockSpec((PEER_HEADS, PEER_NKEYS, tn), lambda i, j: (0, 0, i))
    sel_rows = pl.BlockSpec((PEER_HEADS, PEER_ROWS, tn), lambda i, j: (0, j, i))
    row = pl.BlockSpec((tn, d), lambda i, j: (i, 0))
    return pl.pallas_call(
        functools.partial(_peer_kernel, final=final),
        grid=(n // tn, n_exp // te),
        in_specs=[row, sel_rows, sel, sel, sel_rows,
                  pl.BlockSpec((te, d), lambda i, j: (j, 0)),
                  pl.BlockSpec((d, te), lambda i, j: (0, j)),
                  row, mod_spec, pl.BlockSpec((1, d), lambda i, j: (0, 0))],
        out_specs=row,
        out_shape=jax.ShapeDtypeStruct((n, d), F32),
        scratch_shapes=[pltpu.VMEM((d, tn), F32), pltpu.VMEM((te, tn), F32), pltpu.VMEM((te, tn), BF16)],
        compiler_params=_cparams(("arbitrary", "arbitrary")),
        name="peer",
    )(h2, e1, e2, s2, c, u_b, vt_b, x1, gt2, g_final.reshape(1, d))


def _pad_in_proj(w_in):
    parts, o = [], 0
    for s, ps in zip(SPLIT_SIZES, _PAD_SIZES):
        parts.append(jnp.pad(w_in[:, o:o + s], ((0, 0), (0, ps - s))))
        o += s
    return jnp.concatenate(parts, axis=1).astype(BF16)


def _pad_peer_keys(keys, lo):
    half = keys.shape[-1]
    return jnp.pad(keys, ((0, 0), (0, 0), (lo, PEER_DKEY - half - lo))).astype(BF16)


def _pad_axis(a, axis, size):
    pad = [(0, 0)] * a.ndim
    pad[axis] = (0, size - a.shape[axis])
    return jnp.pad(a, pad)


def _rows_mod(mod, reps):
    parts = jnp.split(mod, 6, axis=-1)
    if mod.shape[0] == 1:
        return parts
    return [jnp.repeat(p, reps, axis=0) for p in parts]


def kernel(x_prompt, x_sample, cache_k, cache_v, cache_kidx, state_gla, page_table, c_prompt, c_sample,
           w_ada, b_ada, g_norm1, g_norm2, w_in, w_gate_a2, b_gate_a2, g_gla_out, g_dsa_out, w_out,
           t5_bias, w_peer_q, peer_keys1, peer_keys2, peer_u, peer_v, g_final):
    depth = w_ada.shape[0]
    B, T, D = x_prompt.shape
    Bd, Ts, _ = x_sample.shape
    xp = x_prompt.reshape(B * T, D)
    xs = x_sample.reshape(Bd * Ts, D)
    R = DSA_S_ROWS
    outs = [[] for _ in range(8)]
    for l in range(depth):
        w_in_p = _pad_in_proj(w_in[l])
        wa2_p = _pad_axis(w_gate_a2[l], 0, LANE).astype(BF16)
        w_out_b = w_out[l].astype(BF16)
        w_pq_b = w_peer_q[l].astype(BF16)
        k1p = _pad_peer_keys(peer_keys1[l], 0)
        k2p = _pad_peer_keys(peer_keys2[l], PEER_DKEY // 2)
        u_b = peer_u[l].astype(BF16)
        vt_b = peer_v[l].T.astype(BF16)
        last = l == depth - 1

        c_all = jnp.concatenate([c_prompt, c_sample], axis=0)
        c_all = _pad_axis(c_all, 0, -(-(B + Bd) // SUBLANE) * SUBLANE)
        mod = _ada(c_all, w_ada[l], b_ada[l])
        mod_p = _rows_mod(mod[:B], T)
        mod_s = _rows_mod(mod[B:B + Bd], Ts)

        sh1, sc1, gt1, sh2, sc2, gt2 = mod_p
        (gq, gk, gv, gg, lg, dq, dk, dv, dkb, dvb, qi, ki, kib, wi) = _proj(
            xp, g_norm1[l], sc1, sh1, w_in_p, wa2_p, b_gate_a2[l], 256)
        s0p = jnp.zeros((B, GLA_HEADS, GLA_DK, GLA_DV), F32)
        r3 = lambda a: a.reshape(B, T, a.shape[-1])
        o_gla, st_p = _gla(r3(gq), r3(gk), r3(gv), r3(gg), r3(lg), g_gla_out[l], s0p, min(GLA_CHUNK, T))
        seq = lambda a, b: a[b * T:(b + 1) * T]
        o_dsa = jnp.concatenate([
            _dsa_prompt(seq(qi, b).T, seq(wi, b)[:, :IDX_HEADS].T, seq(kib, b),
                        seq(dq, b).T, seq(dkb, b), seq(dvb, b).T, t5_bias, g_dsa_out[l])
            for b in range(B)], axis=0)
        x1, h2, pq = _post(o_gla.reshape(B * T, GLA_W), o_dsa, xp, gt1, g_norm2[l], sc2, sh2, w_out_b, w_pq_b, 256)
        e1, e2, s2, cth = _peer_sel(pq, k1p, k2p, 256)
        xp = _peer(h2, e1, e2, s2, cth, u_b, vt_b, x1, gt2, g_final, last)
        outs[0].append(dk.reshape(B, T, DSA_HEADS, DSA_DH))
        outs[1].append(dv.reshape(B, T, DSA_HEADS, DSA_DH))
        outs[2].append(ki.reshape(B, T, IDX_DH))
        outs[3].append(st_p)

        sh1, sc1, gt1, sh2, sc2, gt2 = mod_s
        (gq, gk, gv, gg, lg, dq, dk, dv, dkb, dvb, qi, ki, kib, wi) = _proj(
            xs, g_norm1[l], sc1, sh1, w_in_p, wa2_p, b_gate_a2[l], 256)
        tokpad = lambda a, n: _pad_axis(a.reshape(Bd, Ts, a.shape[-1]), 1, n)
        o_gla, st_s = _gla(tokpad(gq, R), tokpad(gk, R), tokpad(gv, R), tokpad(gg, R), tokpad(lg, R),
                           g_gla_out[l], state_gla[l], R)
        qi_s = _pad_axis(qi.reshape(Bd, Ts, IDX_HEADS, IDX_DH).transpose(0, 2, 1, 3), 2, R)
        wi_s = _pad_axis(wi[:, :IDX_HEADS].reshape(Bd, Ts, IDX_HEADS).transpose(0, 2, 1), 2, R)
        o_dsa = _dsa_sample(page_table, qi_s.reshape(Bd, IDX_HEADS * R, IDX_DH),
                            wi_s.reshape(Bd, IDX_HEADS * R, 1), tokpad(kib, LANE), tokpad(dq, R),
                            tokpad(dkb, LANE), tokpad(dvb, LANE),
                            cache_kidx, cache_k, cache_v, t5_bias, g_dsa_out[l], Ts, l)
        x1, h2, pq = _post(o_gla[:, :Ts].reshape(Bd * Ts, GLA_W), o_dsa[:, :Ts].reshape(Bd * Ts, DSA_W),
                           xs, gt1, g_norm2[l], sc2, sh2, w_out_b, w_pq_b, 256)
        e1, e2, s2, cth = _peer_sel(pq, k1p, k2p, 256)
        xs = _peer(h2, e1, e2, s2, cth, u_b, vt_b, x1, gt2, g_final, last)
        outs[4].append(dk.reshape(Bd, Ts, DSA_HEADS, DSA_DH))
        outs[5].append(dv.reshape(Bd, Ts, DSA_HEADS, DSA_DH))
        outs[6].append(ki.reshape(Bd, Ts, IDX_DH))
        outs[7].append(st_s)

    return (xp.reshape(B, T, D), xs.reshape(Bd, Ts, D)) + tuple(jnp.stack(o) for o in outs)
```
